```python
import math
import jax, jax.numpy as jnp
from jax import lax
import numpy as np

D_MODEL = 2048
BATCH = 4
SEQ = 4096
DEPTH = 1

CHUNK = 64
Q_BLOCK = 128
HEAD_DIM = 128
SB_HEADS = 8
DN_HEADS = 8
SB_WIDTH = SB_HEADS * HEAD_DIM
DN_WIDTH = DN_HEADS * HEAD_DIM
MIX_WIDTH = SB_WIDTH + DN_WIDTH
SHORT_CONV = 4
FFN_CONV = 3
D_FF = 5632
IN_COLS = 3 * SB_WIDTH + 4 * DN_WIDTH + 2 * DN_HEADS
EPS = 1e-6

kernel_name = "sb_gdn_hybrid_convffn_block"


def rmsnorm(x, gain):
    xf = x.astype(jnp.float32)
    y = xf * lax.rsqrt(jnp.mean(xf * xf, axis=-1, keepdims=True) + EPS)
    return (y * gain.astype(jnp.float32)).astype(x.dtype)


def l2norm(x):
    xf = x.astype(jnp.float32)
    return xf * lax.rsqrt(jnp.sum(xf * xf, axis=-1, keepdims=True) + EPS)


def causal_dwconv(x, w):
    K = w.shape[0]
    T = x.shape[1]
    xp = jnp.pad(x, ((0, 0), (K - 1, 0), (0, 0)))
    out = xp[:, 0:T] * w[0]
    for j in range(1, K):
        out = out + xp[:, j:j + T] * w[j]
    return out


def _heads(t, n):
    B, T, _ = t.shape
    return t.reshape(B, T, n, HEAD_DIM).transpose(0, 2, 1, 3)


def stick_breaking_attention(q, k, v):
    B, H, T, Dh = q.shape
    scale = Dh ** -0.5
    outs = []
    for blk in range(T // Q_BLOCK):
        q0 = blk * Q_BLOCK
        kend = q0 + Q_BLOCK
        z = jnp.einsum('bhqd,bhkd->bhqk', q[:, :, q0:kend], k[:, :, :kend]).astype(jnp.float32) * scale
        t_idx = q0 + jnp.arange(Q_BLOCK)[:, None]
        s_idx = jnp.arange(kend)[None, :]
        valid = s_idx < t_idx
        log_beta = jax.nn.log_sigmoid(z)
        log_1m = jnp.where(valid, jax.nn.log_sigmoid(-z), 0.0)
        later = lax.cumsum(log_1m, axis=3, reverse=True) - log_1m
        att = jnp.where(valid, jnp.exp(log_beta + later), 0.0)
        outs.append(jnp.einsum('bhqk,bhkd->bhqd', att.astype(v.dtype), v[:, :, :kend]))
    return jnp.concatenate(outs, axis=2)


def gated_delta_rule(q, k, v, g, beta):
    out_dtype = v.dtype
    B, H, T, Dk = q.shape
    Dv = v.shape[-1]
    C = CHUNK
    N = T // C
    q = q.astype(jnp.float32) * (Dk ** -0.5)
    k = k.astype(jnp.float32)
    v = v.astype(jnp.float32)
    beta = beta.astype(jnp.float32)
    q = q.reshape(B, H, N, C, Dk)
    k = k.reshape(B, H, N, C, Dk)
    v = v.reshape(B, H, N, C, Dv)
    beta = beta.reshape(B, H, N, C)
    g = jnp.cumsum(g.astype(jnp.float32).reshape(B, H, N, C), axis=-1)

    causal = jnp.tril(jnp.ones((C, C), dtype=bool))
    strict = jnp.tril(jnp.ones((C, C), dtype=bool), k=-1)
    diff = g[..., :, None] - g[..., None, :]
    decay = jnp.where(causal, jnp.exp(jnp.where(causal, diff, 0.0)), 0.0)

    k_beta = k * beta[..., None]
    v_beta = v * beta[..., None]
    L = jnp.where(strict, jnp.einsum('bhncd,bhnmd->bhncm', k_beta, k) * decay, 0.0)
    eye = jnp.eye(C, dtype=jnp.float32)
    T_mat = lax.linalg.triangular_solve(eye + L, jnp.broadcast_to(eye, L.shape),
                                        left_side=True, lower=True, unit_diagonal=True)
    u = jnp.einsum('bhncm,bhnmd->bhncd', T_mat, v_beta)
    w = jnp.einsum('bhncm,bhnmd->bhncd', T_mat, k_beta * jnp.exp(g)[..., None])
    qk_intra = jnp.where(causal, jnp.einsum('bhncd,bhnmd->bhncm', q, k) * decay, 0.0)
    g_last = g[..., -1]
    k_to_end = k * jnp.exp(g_last[..., None] - g)[..., None]
    q_decay = q * jnp.exp(g)[..., None]

    def step(S, xs):
        u_i, w_i, qa_i, qd_i, kt_i, gl_i = xs
        v_new = u_i - jnp.einsum('bhcd,bhde->bhce', w_i, S)
        o = jnp.einsum('bhcd,bhde->bhce', qd_i, S) + jnp.einsum('bhcm,bhme->bhce', qa_i, v_new)
        S = S * jnp.exp(gl_i)[..., None, None] + jnp.einsum('bhcd,bhce->bhde', kt_i, v_new)
        return S, o

    to_scan = lambda t: jnp.moveaxis(t, 2, 0)
    xs = (to_scan(u), to_scan(w), to_scan(qk_intra), to_scan(q_decay), to_scan(k_to_end),
          jnp.moveaxis(g_last, 2, 0))
    S0 = jnp.zeros((B, H, Dk, Dv), dtype=jnp.float32)
    _, o = lax.scan(step, S0, xs)
    o = jnp.moveaxis(o, 0, 2).reshape(B, H, T, Dv)
    return o.astype(out_dtype)


def token_mixer(xn, w_in, sb_out_gain, dn_conv_w, dn_a_log, dn_dt_bias, dn_out_gain, w_out):
    B, T, _ = xn.shape
    proj = jnp.einsum('btd,dc->btc', xn, w_in)
    sizes = (SB_WIDTH, SB_WIDTH, SB_WIDTH, 3 * DN_WIDTH, DN_WIDTH, DN_HEADS, DN_HEADS)
    offs = np.cumsum(sizes)[:-1].tolist()
    sb_q, sb_k, sb_v, dn_qkv, dn_z, dn_b, dn_a = jnp.split(proj, offs, axis=-1)

    o_sb = stick_breaking_attention(_heads(sb_q, SB_HEADS), _heads(sb_k, SB_HEADS), _heads(sb_v, SB_HEADS))
    o_sb = rmsnorm(o_sb.transpose(0, 2, 1, 3), sb_out_gain).reshape(B, T, SB_WIDTH)

    dn_qkv = jax.nn.silu(causal_dwconv(dn_qkv, dn_conv_w))
    dq, dk, dv = jnp.split(dn_qkv, 3, axis=-1)
    q = l2norm(_heads(dq, DN_HEADS))
    k = l2norm(_heads(dk, DN_HEADS))
    v = _heads(dv, DN_HEADS)
    beta = jax.nn.sigmoid(dn_b.astype(jnp.float32)).transpose(0, 2, 1)
    g = -(jnp.exp(dn_a_log.astype(jnp.float32)) *
          jax.nn.softplus(dn_a.astype(jnp.float32) + dn_dt_bias.astype(jnp.float32))).transpose(0, 2, 1)
    o_dn = gated_delta_rule(q, k, v, g, beta).transpose(0, 2, 1, 3)
    o_dn = rmsnorm(o_dn, dn_out_gain) * jax.nn.silu(dn_z.reshape(B, T, DN_HEADS, HEAD_DIM))
    o_dn = o_dn.reshape(B, T, DN_WIDTH).astype(xn.dtype)

    mix = jnp.concatenate([o_sb.astype(xn.dtype), o_dn], axis=-1)
    return jnp.einsum('btc,cd->btd', mix, w_out)


def conv_ffn(xn, w_up, ffn_conv_w, ffn_conv_b, w_down):
    h = jnp.einsum('btd,df->btf', xn, w_up)
    h = causal_dwconv(h, ffn_conv_w) + ffn_conv_b
    gate, val = jnp.split(h, 2, axis=-1)
    return jnp.einsum('btf,fd->btd', jax.nn.gelu(gate, approximate=True) * val, w_down)


def setup_inputs(seed: int = 0) -> dict:
    key = jax.random.key(seed)
    ks = jax.random.split(key, 20)
    f32 = jnp.float32
    nrm = lambda k, shape, s: jax.random.normal(k, shape, f32) * s
    gain = lambda k, shape: 1.0 + 0.05 * jax.random.normal(k, shape, f32)
    dt = jnp.exp(jax.random.uniform(ks[5], (DEPTH, DN_HEADS), f32, math.log(1e-3), math.log(1e-1)))
    return {
        "x": jax.random.normal(ks[0], (BATCH, SEQ, D_MODEL), f32),
        "w_in": nrm(ks[1], (DEPTH, D_MODEL, IN_COLS), D_MODEL ** -0.5),
        "sb_out_gain": gain(ks[2], (DEPTH, HEAD_DIM)),
        "dn_conv_w": nrm(ks[3], (DEPTH, SHORT_CONV, 3 * DN_WIDTH), SHORT_CONV ** -0.5),
        "dn_a_log": jnp.log(jax.random.uniform(ks[4], (DEPTH, DN_HEADS), f32, 1.0, 16.0)),
        "dn_dt_bias": dt + jnp.log(-jnp.expm1(-dt)),
        "dn_out_gain": gain(ks[6], (DEPTH, HEAD_DIM)),
        "w_out": nrm(ks[7], (DEPTH, MIX_WIDTH, D_MODEL), MIX_WIDTH ** -0.5),
        "ln_mix_pre": gain(ks[8], (DEPTH, D_MODEL)),
        "ln_mix_post": gain(ks[9], (DEPTH, D_MODEL)),
        "w_up": nrm(ks[10], (DEPTH, D_MODEL, 2 * D_FF), D_MODEL ** -0.5),
        "ffn_conv_w": nrm(ks[11], (DEPTH, FFN_CONV, 2 * D_FF), FFN_CONV ** -0.5),
        "ffn_conv_b": nrm(ks[12], (DEPTH, 2 * D_FF), 0.01),
        "w_down": nrm(ks[13], (DEPTH, D_FF, D_MODEL), D_FF ** -0.5),
        "ln_ffn_pre": gain(ks[14], (DEPTH, D_MODEL)),
        "ln_ffn_post": gain(ks[15], (DEPTH, D_MODEL)),
    }


def reference(x, w_in, sb_out_gain, dn_conv_w, dn_a_log, dn_dt_bias, dn_out_gain, w_out,
              ln_mix_pre, ln_mix_post, w_up, ffn_conv_w, ffn_conv_b, w_down, ln_ffn_pre, ln_ffn_post):
    h = x
    for l in range(DEPTH):
        m = token_mixer(rmsnorm(h, ln_mix_pre[l]), w_in[l], sb_out_gain[l], dn_conv_w[l],
                        dn_a_log[l], dn_dt_bias[l], dn_out_gain[l], w_out[l])
        h = h + rmsnorm(m, ln_mix_post[l])
        f = conv_ffn(rmsnorm(h, ln_ffn_pre[l]), w_up[l], ffn_conv_w[l], ffn_conv_b[l], w_down[l])
        h = h + rmsnorm(f, ln_ffn_post[l])
    return h
```

```python
import functools

import jax
import jax.numpy as jnp
from jax import lax
from jax.experimental import pallas as pl
from jax.experimental.pallas import tpu as pltpu

F32 = jnp.float32
BF16 = jnp.bfloat16

HEAD_DIM = 128
SB_HEADS = 8
DN_HEADS = 8
SB_WIDTH = SB_HEADS * HEAD_DIM
DN_WIDTH = DN_HEADS * HEAD_DIM
SHORT_CONV = 4
FFN_CONV = 3
EPS = 1e-6

VMEM_LIMIT_BYTES = 56 * 1024 * 1024
SUBLANES = 8

IN_TM, IN_TN = 1024, 512
SB_TQ = 256
DN_BLOCK = 256
OUT_TM = 512
FFN_TM, FFN_TF = 512, 512


def _rms(x, gain):
    return x * lax.rsqrt(jnp.mean(x * x, axis=-1, keepdims=True) + EPS) * gain


def _dot(a, b):
    return jnp.dot(a, b, preferred_element_type=F32)


def _dot_hi(a, b):
    return jnp.dot(a, b, preferred_element_type=F32, precision=lax.Precision.HIGHEST)


def _dot_nt(a, b):
    return lax.dot_general(a, b, (((1,), (1,)), ((), ())), preferred_element_type=F32)


def _dot_tn(a, b):
    return lax.dot_general(a, b, (((0,), (0,)), ((), ())), preferred_element_type=F32)


def _in_proj_kernel(x_ref, gain_ref, w_ref, wba_ref, alog_ref, dtb_ref,
                    proj_ref, gb_ref, gbt_ref, xn_ref, *, tm):
    j = pl.program_id(1)

    @pl.when(j == 0)
    def _():
        xn = _rms(x_ref[...], gain_ref[...]).astype(BF16)
        xn_ref[...] = xn
        ba = _dot(xn, wba_ref[...])
        lane = lax.broadcasted_iota(jnp.int32, ba.shape, 1)
        beta = jax.nn.sigmoid(ba)
        g = -(jnp.exp(alog_ref[...]) * jax.nn.softplus(ba + dtb_ref[...]))
        nb = tm // DN_BLOCK
        r = lax.broadcasted_iota(jnp.int32, (DN_BLOCK, DN_BLOCK), 0)
        c = lax.broadcasted_iota(jnp.int32, (DN_BLOCK, DN_BLOCK), 1)
        tri = jnp.broadcast_to((r >= c).astype(F32)[None], (nb, DN_BLOCK, DN_BLOCK))
        gc = jnp.einsum('cij,cjd->cid', tri, g.reshape(nb, DN_BLOCK, 128),
                        preferred_element_type=F32, precision=lax.Precision.HIGHEST)
        gb = jnp.where(lane < DN_HEADS, beta, gc.reshape(tm, 128))
        gb_ref[...] = gb
        gbt_ref[...] = gb.T

    proj_ref[...] = _dot(xn_ref[...], w_ref[...]).astype(proj_ref.dtype)


def _in_proj(x2d, gain, w_main, w_ba, alog, dtb):
    m, d = x2d.shape
    n = w_main.shape[1]
    tm, tn = IN_TM, IN_TN
    return pl.pallas_call(
        functools.partial(_in_proj_kernel, tm=tm),
        grid=(m // tm, n // tn),
        in_specs=[
            pl.BlockSpec((tm, d), lambda i, j: (i, 0)),
            pl.BlockSpec((1, d), lambda i, j: (0, 0)),
            pl.BlockSpec((d, tn), lambda i, j: (0, j)),
            pl.BlockSpec((d, 128), lambda i, j: (0, 0)),
            pl.BlockSpec((1, 128), lambda i, j: (0, 0)),
            pl.BlockSpec((1, 128), lambda i, j: (0, 0)),
        ],
        out_specs=[
            pl.BlockSpec((tm, tn), lambda i, j: (i, j)),
            pl.BlockSpec((tm, 128), lambda i, j: (i, 0)),
            pl.BlockSpec((128, tm), lambda i, j: (0, i)),
        ],
        out_shape=[
            jax.ShapeDtypeStruct((m, n), BF16),
            jax.ShapeDtypeStruct((m, 128), F32),
            jax.ShapeDtypeStruct((128, m), F32),
        ],
        scratch_shapes=[pltpu.VMEM((tm, d), BF16)],
        compiler_params=pltpu.CompilerParams(
            dimension_semantics=("arbitrary", "arbitrary"), vmem_limit_bytes=VMEM_LIMIT_BYTES),
        name="in_proj",
    )(x2d, gain, w_main, w_ba, alog, dtb)


def _sb_kernel(q_ref, k_ref, v_ref, gain_ref, o_ref, *, tq, scale):
    qi = pl.program_id(2)
    q = q_ref[...]
    r = lax.broadcasted_iota(jnp.int32, (tq, tq), 0)
    c = lax.broadcasted_iota(jnp.int32, (tq, tq), 1)
    suffix = (r >= c).astype(BF16)

    def suffix_sum(sp):
        hi = sp.astype(BF16)
        lo = (sp - hi.astype(F32)).astype(BF16)
        return _dot(hi, suffix) + _dot(lo, suffix)

    def scores(kb):
        k = k_ref[pl.ds(kb * tq, tq), :]
        z = _dot_nt(q, k) * scale
        sp = jnp.maximum(z, 0.0) + jnp.log1p(jnp.exp(-jnp.abs(z)))
        return z, sp

    valid = c < r
    z, sp = scores(qi)
    cs = suffix_sum(jnp.where(valid, sp, 0.0))
    att = jnp.where(valid, jnp.exp(z - cs), 0.0)
    acc0 = _dot(att.astype(BF16), v_ref[pl.ds(qi * tq, tq), :])
    run0 = cs[:, 0:1]

    def body(step, carry):
        acc, run = carry
        kb = qi - 1 - step
        z, sp = scores(kb)
        cs = suffix_sum(sp)
        att = jnp.exp(z - cs - run)
        acc = acc + _dot(att.astype(BF16), v_ref[pl.ds(kb * tq, tq), :])
        return acc, run + cs[:, 0:1]

    acc, _ = lax.fori_loop(0, qi, body, (acc0, run0))
    o_ref[...] = _rms(acc, gain_ref[...]).astype(o_ref.dtype)


def _sb_attention(proj3, gain, *, q_col, k_col, v_col):
    b, t, _ = proj3.shape
    tq = SB_TQ
    return pl.pallas_call(
        functools.partial(_sb_kernel, tq=tq, scale=HEAD_DIM ** -0.5),
        grid=(b, SB_HEADS, t // tq),
        in_specs=[
            pl.BlockSpec((None, tq, HEAD_DIM), lambda bi, h, i: (bi, i, q_col + h)),
            pl.BlockSpec((None, t, HEAD_DIM), lambda bi, h, i: (bi, 0, k_col + h)),
            pl.BlockSpec((None, t, HEAD_DIM), lambda bi, h, i: (bi, 0, v_col + h)),
            pl.BlockSpec((1, HEAD_DIM), lambda bi, h, i: (0, 0)),
        ],
        out_specs=pl.BlockSpec((None, tq, HEAD_DIM), lambda bi, h, i: (bi, i, h)),
        out_shape=jax.ShapeDtypeStruct((b, t, SB_WIDTH), BF16),
        compiler_params=pltpu.CompilerParams(
            dimension_semantics=("arbitrary", "arbitrary", "arbitrary"),
            vmem_limit_bytes=VMEM_LIMIT_BYTES),
        name="sb_attn",
    )(proj3, proj3, proj3, gain)


def _dn_kernel(q_ref, k_ref, v_ref, z_ref, wq_ref, wk_ref, wv_ref, gb_ref, gbt_ref, gain_ref,
               o_ref, s_ref, bq_ref, bk_ref, bv_ref, *, blk):
    h = pl.program_id(1)
    i = pl.program_id(2)
    halo = SUBLANES

    @pl.when(i == 0)
    def _():
        s_ref[...] = jnp.zeros_like(s_ref)
        for buf in (bq_ref, bk_ref, bv_ref):
            buf[0:halo, :] = jnp.zeros((halo, HEAD_DIM), F32)

    def conv_silu(x_ref, w_ref, buf_ref):
        x = x_ref[...].astype(F32)
        buf_ref[halo:halo + blk, :] = x
        w = w_ref[...]
        y = x * w[3:4, :]
        for j in range(SHORT_CONV - 1):
            y = y + buf_ref[pl.ds(halo - (SHORT_CONV - 1) + j, blk), :] * w[j:j + 1, :]
        buf_ref[0:halo, :] = x[blk - halo:blk, :]
        return y * jax.nn.sigmoid(y)

    def l2n(x):
        return x * lax.rsqrt(jnp.sum(x * x, axis=-1, keepdims=True) + EPS)

    q = l2n(conv_silu(q_ref, wq_ref, bq_ref)) * (HEAD_DIM ** -0.5)
    k = l2n(conv_silu(k_ref, wk_ref, bk_ref))
    v = conv_silu(v_ref, wv_ref, bv_ref)

    gb = gb_ref[...]
    lane = lax.broadcasted_iota(jnp.int32, gb.shape, 1)
    beta = jnp.sum(jnp.where(lane == h, gb, 0.0), axis=1, keepdims=True)
    g_col = jnp.sum(jnp.where(lane == h + DN_HEADS, gb, 0.0), axis=1, keepdims=True)
    gbt = gbt_ref[...]
    sub = lax.broadcasted_iota(jnp.int32, gbt.shape, 0)
    g_row = jnp.sum(jnp.where(sub == h + DN_HEADS, gbt, 0.0), axis=0, keepdims=True)
    g_last = g_row[:, blk - 1:blk]

    r = lax.broadcasted_iota(jnp.int32, (blk, blk), 0)
    c = lax.broadcasted_iota(jnp.int32, (blk, blk), 1)
    causal = r >= c
    decay = jnp.where(causal, jnp.exp(jnp.where(causal, g_col - g_row, 0.0)), 0.0)

    k_b = k.astype(BF16)
    kb = k * beta
    a = jnp.where(r > c, _dot_nt(kb.astype(BF16), k_b) * decay, 0.0)
    qa = _dot_nt(q.astype(BF16), k_b) * decay

    rc = r ^ c
    t = jnp.where(r == c, 1.0, 0.0) - jnp.where(rc == 1, a, 0.0)
    m = 2
    while m < blk:
        cross = (rc >= m) & (rc < 2 * m)
        am = jnp.where(cross, a, 0.0)
        t = t - _dot_hi(t, _dot_hi(am, t))
        m *= 2

    eg = jnp.exp(g_col)
    t_b = t.astype(BF16)
    u = _dot(t_b, (v * beta).astype(BF16))
    w = _dot(t_b, (kb * eg).astype(BF16))
    qd = q * eg
    kt = k * jnp.exp(g_last - g_col)

    s = s_ref[...]
    s_b = s.astype(BF16)
    v_new = u - _dot(w.astype(BF16), s_b)
    vn_b = v_new.astype(BF16)
    o = _dot(qd.astype(BF16), s_b) + _dot(qa.astype(BF16), vn_b)
    s_ref[...] = s * jnp.exp(g_last) + _dot_tn(kt.astype(BF16), vn_b)

    zz = z_ref[...].astype(F32)
    o_ref[...] = (_rms(o, gain_ref[...]) * (zz * jax.nn.sigmoid(zz))).astype(o_ref.dtype)


def _dn_scan(proj3, conv_w, gb3, gbt, gain, *, q_col, k_col, v_col, z_col):
    b, t, _ = proj3.shape
    blk = DN_BLOCK
    nblk = t // blk
    tok = lambda col: pl.BlockSpec((None, blk, HEAD_DIM), lambda bi, h, i: (bi, i, col + h))
    cw = lambda col: pl.BlockSpec((SHORT_CONV, HEAD_DIM), lambda bi, h, i: (0, col + h))
    return pl.pallas_call(
        functools.partial(_dn_kernel, blk=blk),
        grid=(b, DN_HEADS, nblk),
        in_specs=[
            tok(q_col), tok(k_col), tok(v_col), tok(z_col),
            cw(0), cw(DN_HEADS), cw(2 * DN_HEADS),
            pl.BlockSpec((None, blk, 128), lambda bi, h, i: (bi, i, 0)),
            pl.BlockSpec((128, blk), lambda bi, h, i: (0, bi * nblk + i)),
            pl.BlockSpec((1, HEAD_DIM), lambda bi, h, i: (0, 0)),
        ],
        out_specs=pl.BlockSpec((None, blk, HEAD_DIM), lambda bi, h, i: (bi, i, h)),
        out_shape=jax.ShapeDtypeStruct((b, t, DN_WIDTH), BF16),
        scratch_shapes=[
            pltpu.VMEM((HEAD_DIM, HEAD_DIM), F32),
            pltpu.VMEM((blk + SUBLANES, HEAD_DIM), F32),
            pltpu.VMEM((blk + SUBLANES, HEAD_DIM), F32),
            pltpu.VMEM((blk + SUBLANES, HEAD_DIM), F32),
        ],
        compiler_params=pltpu.CompilerParams(
            dimension_semantics=("arbitrary", "arbitrary", "arbitrary"),
            vmem_limit_bytes=VMEM_LIMIT_BYTES),
        name="dn_scan",
    )(proj3, proj3, proj3, proj3, conv_w, conv_w, conv_w, gb3, gbt, gain)


def _out_proj_kernel(osb_ref, odn_ref, w1_ref, w2_ref, x_ref, gpost_ref, gpre_ref, h_ref, xn_ref):
    m = _dot(osb_ref[...], w1_ref[...]) + _dot(odn_ref[...], w2_ref[...])
    hres = x_ref[...] + _rms(m, gpost_ref[...])
    h_ref[...] = hres
    xn_ref[...] = _rms(hres, gpre_ref[...]).astype(xn_ref.dtype)


def _out_proj(o_sb, o_dn, w1, w2, x2d, g_post, g_pre):
    m, d = x2d.shape
    tm = OUT_TM
    row = lambda width: pl.BlockSpec((tm, width), lambda i: (i, 0))
    full = lambda a: pl.BlockSpec(a.shape, lambda i: (0, 0))
    return pl.pallas_call(
        _out_proj_kernel,
        grid=(m // tm,),
        in_specs=[row(SB_WIDTH), row(DN_WIDTH), full(w1), full(w2), row(d), full(g_post), full(g_pre)],
        out_specs=[row(d), row(d)],
        out_shape=[jax.ShapeDtypeStruct((m, d), F32), jax.ShapeDtypeStruct((m, d), BF16)],
        compiler_params=pltpu.CompilerParams(
            dimension_semantics=("arbitrary",), vmem_limit_bytes=VMEM_LIMIT_BYTES),
        name="out_proj",
    )(o_sb, o_dn, w1, w2, x2d, g_post, g_pre)


def _ffn_kernel(xn_ref, wg_ref, wv_ref, cwg_ref, cwv_ref, bg_ref, bv_ref, wd_ref, h_ref, gain_ref,
                y_ref, acc_ref, bufg_ref, bufv_ref, tailg_ref, tailv_ref, *, tm, nf, tiles_per_seq):
    i = pl.program_id(0)
    f = pl.program_id(1)
    halo = SUBLANES
    xn = xn_ref[...]
    seq_start = (i % tiles_per_seq) == 0

    def up_conv(w_ref, cw_ref, b_ref, buf_ref, tail_ref):
        up = _dot(xn, w_ref[...])
        buf_ref[0:halo, :] = jnp.where(seq_start, 0.0, tail_ref[f])
        buf_ref[halo:halo + tm, :] = up
        cw = cw_ref[...]
        out = up * cw[2:3, :] + b_ref[...]
        for j in range(FFN_CONV - 1):
            out = out + buf_ref[pl.ds(halo - (FFN_CONV - 1) + j, tm), :] * cw[j:j + 1, :]
        tail_ref[f] = up[tm - halo:tm, :]
        return out

    gate = up_conv(wg_ref, cwg_ref, bg_ref, bufg_ref, tailg_ref)
    val = up_conv(wv_ref, cwv_ref, bv_ref, bufv_ref, tailv_ref)
    act = (jax.nn.gelu(gate, approximate=True) * val).astype(BF16)
    contrib = _dot(act, wd_ref[...])

    @pl.when(f == 0)
    def _():
        acc_ref[...] = contrib

    @pl.when(f > 0)
    def _():
        acc_ref[...] += contrib

    @pl.when(f == nf - 1)
    def _():
        y_ref[...] = h_ref[...] + _rms(acc_ref[...], gain_ref[...])


def _conv_ffn(xn2, w_up, conv_w, conv_b, w_down, hres, gain, *, seq_len):
    m, d = hres.shape
    d_ff = w_down.shape[0]
    tm, tf = FFN_TM, FFN_TF
    nf = d_ff // tf
    tok = lambda i, f: (i, 0)
    return pl.pallas_call(
        functools.partial(_ffn_kernel, tm=tm, nf=nf, tiles_per_seq=seq_len // tm),
        grid=(m // tm, nf),
        in_specs=[
            pl.BlockSpec((tm, d), tok),
            pl.BlockSpec((d, tf), lambda i, f: (0, f)),
            pl.BlockSpec((d, tf), lambda i, f: (0, nf + f)),
            pl.BlockSpec((FFN_CONV, tf), lambda i, f: (0, f)),
            pl.BlockSpec((FFN_CONV, tf), lambda i, f: (0, nf + f)),
            pl.BlockSpec((1, tf), lambda i, f: (0, f)),
            pl.BlockSpec((1, tf), lambda i, f: (0, nf + f)),
            pl.BlockSpec((tf, d), lambda i, f: (f, 0)),
            pl.BlockSpec((tm, d), tok),
            pl.BlockSpec((1, d), lambda i, f: (0, 0)),
        ],
        out_specs=pl.BlockSpec((tm, d), tok),
        out_shape=jax.ShapeDtypeStruct((m, d), F32),
        scratch_shapes=[
            pltpu.VMEM((tm, d), F32),
            pltpu.VMEM((tm + SUBLANES, tf), F32),
            pltpu.VMEM((tm + SUBLANES, tf), F32),
            pltpu.VMEM((nf, SUBLANES, tf), F32),
            pltpu.VMEM((nf, SUBLANES, tf), F32),
        ],
        compiler_params=pltpu.CompilerParams(
            dimension_semantics=("arbitrary", "arbitrary"), vmem_limit_bytes=VMEM_LIMIT_BYTES),
        name="conv_ffn",
    )(xn2, w_up, w_up, conv_w, conv_w, conv_b, conv_b, w_down, hres, gain)


def _pad_lanes(vec):
    return jnp.pad(vec.astype(F32), (0, 128 - vec.shape[0]))[None, :]


def kernel(x, w_in, sb_out_gain, dn_conv_w, dn_a_log, dn_dt_bias, dn_out_gain, w_out, ln_mix_pre,
           ln_mix_post, w_up, ffn_conv_w, ffn_conv_b, w_down, ln_ffn_pre, ln_ffn_post):
    b, t, d = x.shape
    depth = w_in.shape[0]
    n_main = 3 * SB_WIDTH + 4 * DN_WIDTH
    assert t % DN_BLOCK == 0 and t % SB_TQ == 0 and t % FFN_TM == 0
    assert (b * t) % IN_TM == 0 and n_main % IN_TN == 0 and IN_TM % DN_BLOCK == 0
    hcols = lambda off: off // HEAD_DIM

    h2d = x.reshape(b * t, d)
    for l in range(depth):
        w_main = w_in[l, :, :n_main].astype(BF16)
        w_ba = jnp.pad(w_in[l, :, n_main:], ((0, 0), (0, 128 - 2 * DN_HEADS))).astype(BF16)
        alog = _pad_lanes(jnp.concatenate([jnp.zeros((DN_HEADS,), F32), dn_a_log[l]]))
        dtb = _pad_lanes(jnp.concatenate([jnp.zeros((DN_HEADS,), F32), dn_dt_bias[l]]))
        proj, gb, gbt = _in_proj(h2d, ln_mix_pre[l][None, :], w_main, w_ba, alog, dtb)
        proj3 = proj.reshape(b, t, n_main)

        o_sb = _sb_attention(proj3, sb_out_gain[l][None, :],
                             q_col=0, k_col=hcols(SB_WIDTH), v_col=hcols(2 * SB_WIDTH))
        dn0 = 3 * SB_WIDTH
        o_dn = _dn_scan(proj3, dn_conv_w[l], gb.reshape(b, t, 128), gbt, dn_out_gain[l][None, :],
                        q_col=hcols(dn0), k_col=hcols(dn0 + DN_WIDTH),
                        v_col=hcols(dn0 + 2 * DN_WIDTH), z_col=hcols(dn0 + 3 * DN_WIDTH))

        w_o = w_out[l].astype(BF16)
        hres, xn2 = _out_proj(o_sb.reshape(b * t, SB_WIDTH), o_dn.reshape(b * t, DN_WIDTH),
                              w_o[:SB_WIDTH], w_o[SB_WIDTH:], h2d,
                              ln_mix_post[l][None, :], ln_ffn_pre[l][None, :])

        h2d = _conv_ffn(xn2, w_up[l].astype(BF16), ffn_conv_w[l], ffn_conv_b[l][None, :],
                        w_down[l].astype(BF16), hres, ln_ffn_post[l][None, :], seq_len=t)
    return h2d.reshape(b, t, d)
```

```python
import functools

import jax
import jax.numpy as jnp
from jax import lax
from jax.experimental import pallas as pl
from jax.experimental.pallas import tpu as pltpu

F32 = jnp.float32
BF16 = jnp.bfloat16

HEAD_DIM = 128
SB_HEADS = 8
DN_HEADS = 8
SB_WIDTH = SB_HEADS * HEAD_DIM
DN_WIDTH = DN_HEADS * HEAD_DIM
SHORT_CONV = 4
FFN_CONV = 3
EPS = 1e-6

VMEM_LIMIT_BYTES = 56 * 1024 * 1024
SUBLANES = 8

IN_TM, IN_TN = 1024, 512
SB_TQ = 256
SB_HEADS_PER_STEP = 4
LOG2E = 1.4426950408889634
DN_BLOCK = 256
DN_HEADS_PER_STEP = 4
OUT_TM = 512
FFN_TM, FFN_TF = 512, 512


def _rms(x, gain):
    return x * lax.rsqrt(jnp.mean(x * x, axis=-1, keepdims=True) + EPS) * gain


def _dot(a, b):
    return jnp.dot(a, b, preferred_element_type=F32)


def _dot_hi(a, b):
    return jnp.dot(a, b, preferred_element_type=F32, precision=lax.Precision.HIGHEST)


def _dot_nt(a, b):
    return lax.dot_general(a, b, (((1,), (1,)), ((), ())), preferred_element_type=F32)


def _dot_tn(a, b):
    return lax.dot_general(a, b, (((0,), (0,)), ((), ())), preferred_element_type=F32)


def _in_proj_kernel(x_ref, gain_ref, w_ref, wba_ref, alog_ref, dtb_ref,
                    proj_ref, gb_ref, gbt_ref, xn_ref, *, tm, q_tiles):
    j = pl.program_id(1)

    @pl.when(j == 0)
    def _():
        xn = _rms(x_ref[...], gain_ref[...]).astype(BF16)
        xn_ref[...] = xn
        ba = _dot(xn, wba_ref[...])
        lane = lax.broadcasted_iota(jnp.int32, ba.shape, 1)
        beta = jax.nn.sigmoid(ba)
        g = -(jnp.exp(alog_ref[...]) * jax.nn.softplus(ba + dtb_ref[...]))
        nb = tm // DN_BLOCK
        r = lax.broadcasted_iota(jnp.int32, (DN_BLOCK, DN_BLOCK), 0)
        c = lax.broadcasted_iota(jnp.int32, (DN_BLOCK, DN_BLOCK), 1)
        tri = jnp.broadcast_to((r >= c).astype(F32)[None], (nb, DN_BLOCK, DN_BLOCK))
        gc = jnp.einsum('cij,cjd->cid', tri, g.reshape(nb, DN_BLOCK, 128),
                        preferred_element_type=F32, precision=lax.Precision.HIGHEST)
        gb = jnp.where(lane < DN_HEADS, beta, gc.reshape(tm, 128))
        gb_ref[...] = gb
        gbt_ref[...] = gb.T

    col_scale = jnp.where(j < q_tiles, HEAD_DIM ** -0.5, 1.0)
    proj_ref[...] = (_dot(xn_ref[...], w_ref[...]) * col_scale).astype(proj_ref.dtype)


def _in_proj(x2d, gain, w_main, w_ba, alog, dtb):
    m, d = x2d.shape
    n = w_main.shape[1]
    tm, tn = IN_TM, IN_TN
    assert SB_WIDTH % tn == 0
    return pl.pallas_call(
        functools.partial(_in_proj_kernel, tm=tm, q_tiles=SB_WIDTH // tn),
        grid=(m // tm, n // tn),
        in_specs=[
            pl.BlockSpec((tm, d), lambda i, j: (i, 0)),
            pl.BlockSpec((1, d), lambda i, j: (0, 0)),
            pl.BlockSpec((d, tn), lambda i, j: (0, j)),
            pl.BlockSpec((d, 128), lambda i, j: (0, 0)),
            pl.BlockSpec((1, 128), lambda i, j: (0, 0)),
            pl.BlockSpec((1, 128), lambda i, j: (0, 0)),
        ],
        out_specs=[
            pl.BlockSpec((tm, tn), lambda i, j: (i, j)),
            pl.BlockSpec((tm, 128), lambda i, j: (i, 0)),
            pl.BlockSpec((128, tm), lambda i, j: (0, i)),
        ],
        out_shape=[
            jax.ShapeDtypeStruct((m, n), BF16),
            jax.ShapeDtypeStruct((m, 128), F32),
            jax.ShapeDtypeStruct((128, m), F32),
        ],
        scratch_shapes=[pltpu.VMEM((tm, d), BF16)],
        compiler_params=pltpu.CompilerParams(
            dimension_semantics=("arbitrary", "arbitrary"), vmem_limit_bytes=VMEM_LIMIT_BYTES),
        name="in_proj",
    )(x2d, gain, w_main, w_ba, alog, dtb)


def _sb_kernel(q_ref, k_ref, v_ref, gain_ref, o_ref, acc_ref, run_ref, *, tq, nh):
    qi = pl.program_id(2)
    r = lax.broadcasted_iota(jnp.int32, (tq, tq), 0)
    c = lax.broadcasted_iota(jnp.int32, (tq, tq), 1)
    suffix = (r >= c).astype(BF16)
    half = tq // 2

    heads = range(nh)
    cols = [slice(hh * HEAD_DIM, (hh + 1) * HEAD_DIM) for hh in heads]

    def tiles(kb, diag):
        rows = pl.ds(kb * tq, tq)
        zs = [_dot_nt(q_ref[:, cols[hh]], k_ref[rows, cols[hh]]) for hh in heads]
        cs2s = []
        for z in zs:
            sp = jnp.maximum(z, 0.0) + jnp.log(1.0 + jnp.exp2(jnp.abs(z) * (-LOG2E)))
            if diag:
                sp = jnp.where(c < r, sp, 0.0)
            hi = sp.astype(BF16)
            lo = (sp - hi.astype(F32)).astype(BF16)
            cs2s.append(_dot(jnp.concatenate([hi, lo], axis=0), suffix))
        pvs, totals = [], []
        for hh in heads:
            z, cs = zs[hh], cs2s[hh][:tq] + cs2s[hh][tq:]
            if diag:
                att = jnp.where(c < r, jnp.exp(z - cs), 0.0)
            else:
                run = run_ref[hh]
                att = jnp.concatenate(
                    [jnp.exp(z[:, :half] - cs[:, :half] - run),
                     jnp.exp(z[:, half:] - cs[:, half:] - run)], axis=1)
            pvs.append(_dot(att.astype(BF16), v_ref[rows, cols[hh]]))
            totals.append(jnp.broadcast_to(cs[:, 0:1], (tq, HEAD_DIM)))
        for hh in heads:
            if diag:
                acc_ref[hh] = pvs[hh]
                run_ref[hh] = totals[hh]
            else:
                acc_ref[hh] += pvs[hh]
                run_ref[hh] += totals[hh]

    tiles(qi, True)

    def body(step, carry):
        tiles(qi - 1 - step, False)
        return carry

    lax.fori_loop(0, qi, body, 0)
    gain = gain_ref[...]
    for hh in range(nh):
        o_ref[:, hh * HEAD_DIM:(hh + 1) * HEAD_DIM] = _rms(acc_ref[hh], gain).astype(o_ref.dtype)


def _sb_attention(proj3, gain, *, q_col, k_col, v_col):
    b, t, _ = proj3.shape
    tq, nh = SB_TQ, SB_HEADS_PER_STEP
    assert tq == 2 * HEAD_DIM
    width = nh * HEAD_DIM
    return pl.pallas_call(
        functools.partial(_sb_kernel, tq=tq, nh=nh),
        grid=(b, SB_HEADS // nh, t // tq),
        in_specs=[
            pl.BlockSpec((None, tq, width), lambda bi, g, i: (bi, i, q_col // nh + g)),
            pl.BlockSpec((None, t, width), lambda bi, g, i: (bi, 0, k_col // nh + g)),
            pl.BlockSpec((None, t, width), lambda bi, g, i: (bi, 0, v_col // nh + g)),
            pl.BlockSpec((1, HEAD_DIM), lambda bi, g, i: (0, 0)),
        ],
        out_specs=pl.BlockSpec((None, tq, width), lambda bi, g, i: (bi, i, g)),
        out_shape=jax.ShapeDtypeStruct((b, t, SB_WIDTH), BF16),
        scratch_shapes=[
            pltpu.VMEM((nh, tq, HEAD_DIM), F32),
            pltpu.VMEM((nh, tq, HEAD_DIM), F32),
        ],
        compiler_params=pltpu.CompilerParams(
            dimension_semantics=("arbitrary", "arbitrary", "arbitrary"),
            vmem_limit_bytes=VMEM_LIMIT_BYTES),
        name="sb_attn",
    )(proj3, proj3, proj3, gain)


def _dn_kernel(q_ref, k_ref, v_ref, z_ref, wq_ref, wk_ref, wv_ref, gb_ref, gbt_ref, gain_ref,
               o_ref, s_ref, bq_ref, bk_ref, bv_ref, *, blk, nh):
    grp = pl.program_id(1)
    i = pl.program_id(2)
    halo = SUBLANES
    heads = range(nh)
    cols = [slice(hh * HEAD_DIM, (hh + 1) * HEAD_DIM) for hh in heads]

    @pl.when(i == 0)
    def _():
        s_ref[...] = jnp.zeros_like(s_ref)
        for buf in (bq_ref, bk_ref, bv_ref):
            buf[0:halo, :] = jnp.zeros((halo, nh * HEAD_DIM), F32)

    def conv_silu(x_ref, w_ref, buf_ref):
        x = x_ref[...].astype(F32)
        buf_ref[halo:halo + blk, :] = x
        w = w_ref[...]
        y = x * w[3:4, :]
        for j in range(SHORT_CONV - 1):
            y = y + buf_ref[pl.ds(halo - (SHORT_CONV - 1) + j, blk), :] * w[j:j + 1, :]
        buf_ref[0:halo, :] = x[blk - halo:blk, :]
        return y * jax.nn.sigmoid(y)

    def l2n(x):
        return x * lax.rsqrt(jnp.sum(x * x, axis=-1, keepdims=True) + EPS)

    qc = conv_silu(q_ref, wq_ref, bq_ref)
    kc = conv_silu(k_ref, wk_ref, bk_ref)
    vc = conv_silu(v_ref, wv_ref, bv_ref)
    q = [l2n(qc[:, cols[hh]]) * (HEAD_DIM ** -0.5) for hh in heads]
    k = [l2n(kc[:, cols[hh]]) for hh in heads]
    v = [vc[:, cols[hh]] for hh in heads]

    gb = gb_ref[...]
    gbt = gbt_ref[...]
    lane = lax.broadcasted_iota(jnp.int32, gb.shape, 1)
    sub = lax.broadcasted_iota(jnp.int32, gbt.shape, 0)
    r = lax.broadcasted_iota(jnp.int32, (blk, blk), 0)
    c = lax.broadcasted_iota(jnp.int32, (blk, blk), 1)
    causal = r >= c

    beta, g_col, g_last, decay = [], [], [], []
    for hh in heads:
        head = grp * nh + hh
        beta.append(jnp.sum(jnp.where(lane == head, gb, 0.0), axis=1, keepdims=True))
        g_col.append(jnp.sum(jnp.where(lane == head + DN_HEADS, gb, 0.0), axis=1, keepdims=True))
        g_row = jnp.sum(jnp.where(sub == head + DN_HEADS, gbt, 0.0), axis=0, keepdims=True)
        g_last.append(g_row[:, blk - 1:blk])
        decay.append(jnp.where(causal, jnp.exp(jnp.where(causal, g_col[hh] - g_row, 0.0)), 0.0))

    k_b = [k[hh].astype(BF16) for hh in heads]
    kb = [k[hh] * beta[hh] for hh in heads]
    kk = [_dot_nt(kb[hh].astype(BF16), k_b[hh]) for hh in heads]
    qk = [_dot_nt(q[hh].astype(BF16), k_b[hh]) for hh in heads]
    a = [jnp.where(r > c, kk[hh] * decay[hh], 0.0) for hh in heads]
    qa = [(qk[hh] * decay[hh]).astype(BF16) for hh in heads]

    rc = r ^ c
    level = 31 - lax.clz(rc)
    eye = jnp.where(r == c, 1.0, 0.0)
    t = [eye - jnp.where(level == 0, a[hh], 0.0) for hh in heads]
    for p in range(1, blk.bit_length() - 1):
        t_b = [t[hh].astype(BF16) for hh in heads]
        cross = level == p
        prod = [_dot(jnp.where(cross, a[hh], 0.0).astype(BF16), t_b[hh]) for hh in heads]
        t = [t[hh] - _dot(t_b[hh], prod[hh].astype(BF16)) for hh in heads]

    eg = [jnp.exp(g_col[hh]) for hh in heads]
    t_b = [t[hh].astype(BF16) for hh in heads]
    u = [_dot(t_b[hh], (v[hh] * beta[hh]).astype(BF16)) for hh in heads]
    w = [_dot(t_b[hh], (kb[hh] * eg[hh]).astype(BF16)) for hh in heads]
    qd = [(q[hh] * eg[hh]).astype(BF16) for hh in heads]
    kt = [(k[hh] * jnp.exp(g_last[hh] - g_col[hh])).astype(BF16) for hh in heads]

    s = [s_ref[hh] for hh in heads]
    s_b = [s[hh].astype(BF16) for hh in heads]
    ws = [_dot(w[hh].astype(BF16), s_b[hh]) for hh in heads]
    qs = [_dot(qd[hh], s_b[hh]) for hh in heads]
    vn_b = [(u[hh] - ws[hh]).astype(BF16) for hh in heads]
    o = [qs[hh] + _dot(qa[hh], vn_b[hh]) for hh in heads]
    for hh in heads:
        s_ref[hh] = s[hh] * jnp.exp(g_last[hh]) + _dot_tn(kt[hh], vn_b[hh])

    gain = gain_ref[...]
    for hh in heads:
        zz = z_ref[:, cols[hh]].astype(F32)
        o_ref[:, cols[hh]] = (_rms(o[hh], gain) * (zz * jax.nn.sigmoid(zz))).astype(o_ref.dtype)


def _dn_scan(proj3, conv_w, gb3, gbt, gain, *, q_col, k_col, v_col, z_col):
    b, t, _ = proj3.shape
    blk, nh = DN_BLOCK, DN_HEADS_PER_STEP
    nblk = t // blk
    width = nh * HEAD_DIM
    tok = lambda col: pl.BlockSpec((None, blk, width), lambda bi, g, i: (bi, i, col // nh + g))
    cw = lambda col: pl.BlockSpec((SHORT_CONV, width), lambda bi, g, i: (0, col // nh + g))
    return pl.pallas_call(
        functools.partial(_dn_kernel, blk=blk, nh=nh),
        grid=(b, DN_HEADS // nh, nblk),
        in_specs=[
            tok(q_col), tok(k_col), tok(v_col), tok(z_col),
            cw(0), cw(DN_HEADS), cw(2 * DN_HEADS),
            pl.BlockSpec((None, blk, 128), lambda bi, g, i: (bi, i, 0)),
            pl.BlockSpec((128, blk), lambda bi, g, i: (0, bi * nblk + i)),
            pl.BlockSpec((1, HEAD_DIM), lambda bi, g, i: (0, 0)),
        ],
        out_specs=pl.BlockSpec((None, blk, width), lambda bi, g, i: (bi, i, g)),
        out_shape=jax.ShapeDtypeStruct((b, t, DN_WIDTH), BF16),
        scratch_shapes=[
            pltpu.VMEM((nh, HEAD_DIM, HEAD_DIM), F32),
            pltpu.VMEM((blk + SUBLANES, width), F32),
            pltpu.VMEM((blk + SUBLANES, width), F32),
            pltpu.VMEM((blk + SUBLANES, width), F32),
        ],
        compiler_params=pltpu.CompilerParams(
            dimension_semantics=("arbitrary", "arbitrary", "arbitrary"),
            vmem_limit_bytes=VMEM_LIMIT_BYTES),
        name="dn_scan",
    )(proj3, proj3, proj3, proj3, conv_w, conv_w, conv_w, gb3, gbt, gain)


def _out_proj_kernel(osb_ref, odn_ref, w1_ref, w2_ref, x_ref, gpost_ref, gpre_ref, h_ref, xn_ref):
    m = _dot(osb_ref[...], w1_ref[...]) + _dot(odn_ref[...], w2_ref[...])
    hres = x_ref[...] + _rms(m, gpost_ref[...])
    h_ref[...] = hres
    xn_ref[...] = _rms(hres, gpre_ref[...]).astype(xn_ref.dtype)


def _out_proj(o_sb, o_dn, w1, w2, x2d, g_post, g_pre):
    m, d = x2d.shape
    tm = OUT_TM
    row = lambda width: pl.BlockSpec((tm, width), lambda i: (i, 0))
    full = lambda a: pl.BlockSpec(a.shape, lambda i: (0, 0))
    return pl.pallas_call(
        _out_proj_kernel,
        grid=(m // tm,),
        in_specs=[row(SB_WIDTH), row(DN_WIDTH), full(w1), full(w2), row(d), full(g_post), full(g_pre)],
        out_specs=[row(d), row(d)],
        out_shape=[jax.ShapeDtypeStruct((m, d), F32), jax.ShapeDtypeStruct((m, d), BF16)],
        compiler_params=pltpu.CompilerParams(
            dimension_semantics=("arbitrary",), vmem_limit_bytes=VMEM_LIMIT_BYTES),
        name="out_proj",
    )(o_sb, o_dn, w1, w2, x2d, g_post, g_pre)


def _ffn_kernel(xn_ref, wg_ref, wv_ref, cwg_ref, cwv_ref, bg_ref, bv_ref, wd_ref, h_ref, gain_ref,
                y_ref, acc_ref, bufg_ref, bufv_ref, tailg_ref, tailv_ref, *, tm, nf, tiles_per_seq):
    i = pl.program_id(0)
    f = pl.program_id(1)
    halo = SUBLANES
    xn = xn_ref[...]
    seq_start = (i % tiles_per_seq) == 0

    def up_conv(w_ref, cw_ref, b_ref, buf_ref, tail_ref):
        up = _dot(xn, w_ref[...])
        buf_ref[0:halo, :] = jnp.where(seq_start, 0.0, tail_ref[f])
        buf_ref[halo:halo + tm, :] = up
        cw = cw_ref[...]
        out = up * cw[2:3, :] + b_ref[...]
        for j in range(FFN_CONV - 1):
            out = out + buf_ref[pl.ds(halo - (FFN_CONV - 1) + j, tm), :] * cw[j:j + 1, :]
        tail_ref[f] = up[tm - halo:tm, :]
        return out

    gate = up_conv(wg_ref, cwg_ref, bg_ref, bufg_ref, tailg_ref)
    val = up_conv(wv_ref, cwv_ref, bv_ref, bufv_ref, tailv_ref)
    act = (jax.nn.gelu(gate, approximate=True) * val).astype(BF16)
    contrib = _dot(act, wd_ref[...])

    @pl.when(f == 0)
    def _():
        acc_ref[...] = contrib

    @pl.when(f > 0)
    def _():
        acc_ref[...] += contrib

    @pl.when(f == nf - 1)
    def _():
        y_ref[...] = h_ref[...] + _rms(acc_ref[...], gain_ref[...])


def _conv_ffn(xn2, w_up, conv_w, conv_b, w_down, hres, gain, *, seq_len):
    m, d = hres.shape
    d_ff = w_down.shape[0]
    tm, tf = FFN_TM, FFN_TF
    nf = d_ff // tf
    tok = lambda i, f: (i, 0)
    return pl.pallas_call(
        functools.partial(_ffn_kernel, tm=tm, nf=nf, tiles_per_seq=seq_len // tm),
        grid=(m // tm, nf),
        in_specs=[
            pl.BlockSpec((tm, d), tok),
            pl.BlockSpec((d, tf), lambda i, f: (0, f)),
            pl.BlockSpec((d, tf), lambda i, f: (0, nf + f)),
            pl.BlockSpec((FFN_CONV, tf), lambda i, f: (0, f)),
            pl.BlockSpec((FFN_CONV, tf), lambda i, f: (0, nf + f)),
            pl.BlockSpec((1, tf), lambda i, f: (0, f)),
            pl.BlockSpec((1, tf), lambda i, f: (0, nf + f)),
            pl.BlockSpec((tf, d), lambda i, f: (f, 0)),
            pl.BlockSpec((tm, d), tok),
            pl.BlockSpec((1, d), lambda i, f: (0, 0)),
        ],
        out_specs=pl.BlockSpec((tm, d), tok),
        out_shape=jax.ShapeDtypeStruct((m, d), F32),
        scratch_shapes=[
            pltpu.VMEM((tm, d), F32),
            pltpu.VMEM((tm + SUBLANES, tf), F32),
            pltpu.VMEM((tm + SUBLANES, tf), F32),
            pltpu.VMEM((nf, SUBLANES, tf), F32),
            pltpu.VMEM((nf, SUBLANES, tf), F32),
        ],
        compiler_params=pltpu.CompilerParams(
            dimension_semantics=("arbitrary", "arbitrary"), vmem_limit_bytes=VMEM_LIMIT_BYTES),
        name="conv_ffn",
    )(xn2, w_up, w_up, conv_w, conv_w, conv_b, conv_b, w_down, hres, gain)


def _pad_lanes(vec):
    return jnp.pad(vec.astype(F32), (0, 128 - vec.shape[0]))[None, :]


def kernel(x, w_in, sb_out_gain, dn_conv_w, dn_a_log, dn_dt_bias, dn_out_gain, w_out, ln_mix_pre,
           ln_mix_post, w_up, ffn_conv_w, ffn_conv_b, w_down, ln_ffn_pre, ln_ffn_post):
    b, t, d = x.shape
    depth = w_in.shape[0]
    n_main = 3 * SB_WIDTH + 4 * DN_WIDTH
    assert t % DN_BLOCK == 0 and t % SB_TQ == 0 and t % FFN_TM == 0
    assert (b * t) % IN_TM == 0 and n_main % IN_TN == 0 and IN_TM % DN_BLOCK == 0
    hcols = lambda off: off // HEAD_DIM

    h2d = x.reshape(b * t, d)
    for l in range(depth):
        w_main = w_in[l, :, :n_main].astype(BF16)
        w_ba = jnp.pad(w_in[l, :, n_main:], ((0, 0), (0, 128 - 2 * DN_HEADS))).astype(BF16)
        alog = _pad_lanes(jnp.concatenate([jnp.zeros((DN_HEADS,), F32), dn_a_log[l]]))
        dtb = _pad_lanes(jnp.concatenate([jnp.zeros((DN_HEADS,), F32), dn_dt_bias[l]]))
        proj, gb, gbt = _in_proj(h2d, ln_mix_pre[l][None, :], w_main, w_ba, alog, dtb)
        proj3 = proj.reshape(b, t, n_main)

        o_sb = _sb_attention(proj3, sb_out_gain[l][None, :],
                             q_col=0, k_col=hcols(SB_WIDTH), v_col=hcols(2 * SB_WIDTH))
        dn0 = 3 * SB_WIDTH
        o_dn = _dn_scan(proj3, dn_conv_w[l], gb.reshape(b, t, 128), gbt, dn_out_gain[l][None, :],
                        q_col=hcols(dn0), k_col=hcols(dn0 + DN_WIDTH),
                        v_col=hcols(dn0 + 2 * DN_WIDTH), z_col=hcols(dn0 + 3 * DN_WIDTH))

        w_o = w_out[l].astype(BF16)
        hres, xn2 = _out_proj(o_sb.reshape(b * t, SB_WIDTH), o_dn.reshape(b * t, DN_WIDTH),
                              w_o[:SB_WIDTH], w_o[SB_WIDTH:], h2d,
                              ln_mix_post[l][None, :], ln_ffn_pre[l][None, :])

        h2d = _conv_ffn(xn2, w_up[l].astype(BF16), ffn_conv_w[l], ffn_conv_b[l][None, :],
                        w_down[l].astype(BF16), hres, ln_ffn_post[l][None, :], seq_len=t)
    return h2d.reshape(b, t, d)
```

```python
import functools

import jax
import jax.numpy as jnp
from jax import lax
from jax.experimental import pallas as pl
from jax.experimental.pallas import tpu as pltpu

F32 = jnp.float32
BF16 = jnp.bfloat16

HEAD_DIM = 128
SB_HEADS = 8
DN_HEADS = 8
SB_WIDTH = SB_HEADS * HEAD_DIM
DN_WIDTH = DN_HEADS * HEAD_DIM
SHORT_CONV = 4
FFN_CONV = 3
EPS = 1e-6

VMEM_LIMIT_BYTES = 56 * 1024 * 1024
SUBLANES = 8

IN_TM, IN_TN = 1024, 1792
SB_TQ = 256
SB_HEADS_PER_STEP = 8
LOG2E = 1.4426950408889634
DN_BLOCK = 256
DN_HEADS_PER_STEP = 8
OUT_TM = 512
FFN_TM, FFN_TF = 512, 512
FFN_ROW_PARTS = 2


def _rms(x, gain):
    return x * lax.rsqrt(jnp.mean(x * x, axis=-1, keepdims=True) + EPS) * gain


def _dot(a, b):
    return jnp.dot(a, b, preferred_element_type=F32)


def _dot_nt(a, b):
    return lax.dot_general(a, b, (((1,), (1,)), ((), ())), preferred_element_type=F32)


def _dot_tn(a, b):
    return lax.dot_general(a, b, (((0,), (0,)), ((), ())), preferred_element_type=F32)


def _in_proj_kernel(x_ref, gain_ref, w_ref, cscale_ref, wba_ref, alog_ref, dtb_ref,
                    proj_ref, gb_ref, gbt_ref, xn_ref, *, tm):
    j = pl.program_id(1)

    @pl.when(j == 0)
    def _():
        xn = _rms(x_ref[...], gain_ref[...]).astype(BF16)
        xn_ref[...] = xn
        ba = _dot(xn, wba_ref[...])
        lane = lax.broadcasted_iota(jnp.int32, ba.shape, 1)
        beta = jax.nn.sigmoid(ba)
        g = -(jnp.exp(alog_ref[...]) * jax.nn.softplus(ba + dtb_ref[...]))
        r = lax.broadcasted_iota(jnp.int32, (DN_BLOCK, DN_BLOCK), 0)
        c = lax.broadcasted_iota(jnp.int32, (DN_BLOCK, DN_BLOCK), 1)
        tri = (r >= c).astype(BF16)
        g1 = g.astype(BF16)
        rem = g - g1.astype(F32)
        g2 = rem.astype(BF16)
        g3 = (rem - g2.astype(F32)).astype(BF16)
        gsplit = jnp.concatenate([g1, g2, g3], axis=1)
        gc = []
        for blk0 in range(0, tm, DN_BLOCK):
            part = _dot(tri, gsplit[blk0:blk0 + DN_BLOCK, :])
            gc.append(part[:, 0:128] + part[:, 128:256] + part[:, 256:384])
        gb = jnp.where(lane < DN_HEADS, beta, jnp.concatenate(gc, axis=0))
        gb_ref[...] = gb
        gbt_ref[...] = gb.T

    proj_ref[...] = (_dot(xn_ref[...], w_ref[...]) * cscale_ref[...]).astype(proj_ref.dtype)


def _in_proj(x2d, gain, w_main, w_ba, alog, dtb):
    m, d = x2d.shape
    n = w_main.shape[1]
    tm, tn = IN_TM, IN_TN
    cscale = jnp.where(jnp.arange(n) < SB_WIDTH, HEAD_DIM ** -0.5, 1.0).astype(F32)[None, :]
    return pl.pallas_call(
        functools.partial(_in_proj_kernel, tm=tm),
        grid=(m // tm, n // tn),
        in_specs=[
            pl.BlockSpec((tm, d), lambda i, j: (i, 0)),
            pl.BlockSpec((1, d), lambda i, j: (0, 0)),
            pl.BlockSpec((d, tn), lambda i, j: (0, j)),
            pl.BlockSpec((1, tn), lambda i, j: (0, j)),
            pl.BlockSpec((d, 128), lambda i, j: (0, 0)),
            pl.BlockSpec((1, 128), lambda i, j: (0, 0)),
            pl.BlockSpec((1, 128), lambda i, j: (0, 0)),
        ],
        out_specs=[
            pl.BlockSpec((tm, tn), lambda i, j: (i, j)),
            pl.BlockSpec((tm, 128), lambda i, j: (i, 0)),
            pl.BlockSpec((128, tm), lambda i, j: (0, i)),
        ],
        out_shape=[
            jax.ShapeDtypeStruct((m, n), BF16),
            jax.ShapeDtypeStruct((m, 128), F32),
            jax.ShapeDtypeStruct((128, m), F32),
        ],
        scratch_shapes=[pltpu.VMEM((tm, d), BF16)],
        compiler_params=pltpu.CompilerParams(
            dimension_semantics=("arbitrary", "arbitrary"), vmem_limit_bytes=VMEM_LIMIT_BYTES),
        name="in_proj",
    )(x2d, gain, w_main, cscale, w_ba, alog, dtb)


def _sb_kernel(q_ref, k_ref, v_ref, gain_ref, o_ref, acc_ref, run_ref, *, tq, nh):
    qi = pl.program_id(2)
    r = lax.broadcasted_iota(jnp.int32, (tq, tq), 0)
    c = lax.broadcasted_iota(jnp.int32, (tq, tq), 1)
    suffix = (r >= c).astype(BF16)
    half = tq // 2

    heads = range(nh)
    cols = [slice(hh * HEAD_DIM, (hh + 1) * HEAD_DIM) for hh in heads]

    def tiles(kb, diag):
        rows = pl.ds(kb * tq, tq)
        zs = [_dot_nt(q_ref[:, cols[hh]], k_ref[rows, cols[hh]]) for hh in heads]
        css = []
        for z in zs:
            sp = jnp.maximum(z, 0.0) + jnp.log(1.0 + jnp.exp2(jnp.abs(z) * (-LOG2E)))
            if diag:
                sp = jnp.where(c < r, sp, 0.0)
            css.append(_dot(sp.astype(BF16), suffix))
        pvs, totals = [], []
        for hh in heads:
            z, cs = zs[hh], css[hh]
            if diag:
                att = jnp.where(c < r, jnp.exp(z - cs), 0.0)
            else:
                run = run_ref[hh]
                att = jnp.concatenate(
                    [jnp.exp(z[:, :half] - cs[:, :half] - run),
                     jnp.exp(z[:, half:] - cs[:, half:] - run)], axis=1)
            pvs.append(_dot(att.astype(BF16), v_ref[rows, cols[hh]]))
            totals.append(jnp.broadcast_to(cs[:, 0:1], (tq, HEAD_DIM)))
        for hh in heads:
            if diag:
                acc_ref[hh] = pvs[hh]
                run_ref[hh] = totals[hh]
            else:
                acc_ref[hh] += pvs[hh]
                run_ref[hh] += totals[hh]

    tiles(qi, True)

    def body(step, carry):
        tiles(qi - 1 - step, False)
        return carry

    lax.fori_loop(0, qi, body, 0)
    gain = gain_ref[...]
    for hh in range(nh):
        o_ref[:, hh * HEAD_DIM:(hh + 1) * HEAD_DIM] = _rms(acc_ref[hh], gain).astype(o_ref.dtype)


def _sb_attention(proj3, gain, *, q_col, k_col, v_col):
    b, t, _ = proj3.shape
    tq, nh = SB_TQ, SB_HEADS_PER_STEP
    assert tq == 2 * HEAD_DIM
    width = nh * HEAD_DIM
    return pl.pallas_call(
        functools.partial(_sb_kernel, tq=tq, nh=nh),
        grid=(b, SB_HEADS // nh, t // tq),
        in_specs=[
            pl.BlockSpec((None, tq, width), lambda bi, g, i: (bi, i, q_col // nh + g)),
            pl.BlockSpec((None, t, width), lambda bi, g, i: (bi, 0, k_col // nh + g)),
            pl.BlockSpec((None, t, width), lambda bi, g, i: (bi, 0, v_col // nh + g)),
            pl.BlockSpec((1, HEAD_DIM), lambda bi, g, i: (0, 0)),
        ],
        out_specs=pl.BlockSpec((None, tq, width), lambda bi, g, i: (bi, i, g)),
        out_shape=jax.ShapeDtypeStruct((b, t, SB_WIDTH), BF16),
        scratch_shapes=[
            pltpu.VMEM((nh, tq, HEAD_DIM), F32),
            pltpu.VMEM((nh, tq, HEAD_DIM), F32),
        ],
        compiler_params=pltpu.CompilerParams(
            dimension_semantics=("arbitrary", "arbitrary", "arbitrary"),
            vmem_limit_bytes=VMEM_LIMIT_BYTES),
        name="sb_attn",
    )(proj3, proj3, proj3, gain)


def _dn_kernel(q_ref, k_ref, v_ref, z_ref, wq_ref, wk_ref, wv_ref, gb_ref, gbt_ref, gain_ref,
               o_ref, s_ref, bq_ref, bk_ref, bv_ref, *, blk, nh):
    grp = pl.program_id(1)
    i = pl.program_id(2)
    halo = SUBLANES
    heads = range(nh)
    cols = [slice(hh * HEAD_DIM, (hh + 1) * HEAD_DIM) for hh in heads]

    @pl.when(i == 0)
    def _():
        s_ref[...] = jnp.zeros_like(s_ref)
        for buf in (bq_ref, bk_ref, bv_ref):
            buf[0:halo, :] = jnp.zeros((halo, nh * HEAD_DIM), F32)

    def conv_silu(x_ref, w_ref, buf_ref):
        x = x_ref[...].astype(F32)
        buf_ref[halo:halo + blk, :] = x
        w = w_ref[...]
        y = x * w[3:4, :]
        for j in range(SHORT_CONV - 1):
            y = y + buf_ref[pl.ds(halo - (SHORT_CONV - 1) + j, blk), :] * w[j:j + 1, :]
        buf_ref[0:halo, :] = x[blk - halo:blk, :]
        return y * jax.nn.sigmoid(y)

    def l2n(x):
        return x * lax.rsqrt(jnp.sum(x * x, axis=-1, keepdims=True) + EPS)

    qc = conv_silu(q_ref, wq_ref, bq_ref)
    kc = conv_silu(k_ref, wk_ref, bk_ref)
    vc = conv_silu(v_ref, wv_ref, bv_ref)
    q = [l2n(qc[:, cols[hh]]) * (HEAD_DIM ** -0.5) for hh in heads]
    k = [l2n(kc[:, cols[hh]]) for hh in heads]
    v = [vc[:, cols[hh]] for hh in heads]

    gb = gb_ref[...]
    gbt = gbt_ref[...]
    lane = lax.broadcasted_iota(jnp.int32, gb.shape, 1)
    sub = lax.broadcasted_iota(jnp.int32, gbt.shape, 0)
    r = lax.broadcasted_iota(jnp.int32, (blk, blk), 0)
    c = lax.broadcasted_iota(jnp.int32, (blk, blk), 1)
    causal = r >= c

    beta, g_col, g_last, decay = [], [], [], []
    for hh in heads:
        head = grp * nh + hh
        beta.append(jnp.sum(jnp.where(lane == head, gb, 0.0), axis=1, keepdims=True))
        g_col.append(jnp.sum(jnp.where(lane == head + DN_HEADS, gb, 0.0), axis=1, keepdims=True))
        g_row = jnp.sum(jnp.where(sub == head + DN_HEADS, gbt, 0.0), axis=0, keepdims=True)
        g_last.append(g_row[:, blk - 1:blk])
        decay.append(jnp.where(causal, jnp.exp(jnp.where(causal, g_col[hh] - g_row, 0.0)), 0.0))

    k_b = [k[hh].astype(BF16) for hh in heads]
    kb = [k[hh] * beta[hh] for hh in heads]
    kk = [_dot_nt(kb[hh].astype(BF16), k_b[hh]) for hh in heads]
    qk = [_dot_nt(q[hh].astype(BF16), k_b[hh]) for hh in heads]
    a = [jnp.where(r > c, kk[hh] * decay[hh], 0.0) for hh in heads]
    qa = [(qk[hh] * decay[hh]).astype(BF16) for hh in heads]

    rc = r ^ c
    level = 31 - lax.clz(rc)
    eye = jnp.where(r == c, 1.0, 0.0)
    t = [eye - jnp.where(level == 0, a[hh], 0.0) for hh in heads]
    for p in range(1, blk.bit_length() - 1):
        t_b = [t[hh].astype(BF16) for hh in heads]
        cross = level == p
        prod = [_dot(jnp.where(cross, a[hh], 0.0).astype(BF16), t_b[hh]) for hh in heads]
        t = [t[hh] - _dot(t_b[hh], prod[hh].astype(BF16)) for hh in heads]

    eg = [jnp.exp(g_col[hh]) for hh in heads]
    t_b = [t[hh].astype(BF16) for hh in heads]
    u = [_dot(t_b[hh], (v[hh] * beta[hh]).astype(BF16)) for hh in heads]
    w = [_dot(t_b[hh], (kb[hh] * eg[hh]).astype(BF16)) for hh in heads]
    qd = [(q[hh] * eg[hh]).astype(BF16) for hh in heads]
    kt = [(k[hh] * jnp.exp(g_last[hh] - g_col[hh])).astype(BF16) for hh in heads]

    s = [s_ref[hh] for hh in heads]
    s_b = [s[hh].astype(BF16) for hh in heads]
    ws = [_dot(w[hh].astype(BF16), s_b[hh]) for hh in heads]
    qs = [_dot(qd[hh], s_b[hh]) for hh in heads]
    vn_b = [(u[hh] - ws[hh]).astype(BF16) for hh in heads]
    o = [qs[hh] + _dot(qa[hh], vn_b[hh]) for hh in heads]
    for hh in heads:
        s_ref[hh] = s[hh] * jnp.exp(g_last[hh]) + _dot_tn(kt[hh], vn_b[hh])

    gain = gain_ref[...]
    for hh in heads:
        zz = z_ref[:, cols[hh]].astype(F32)
        o_ref[:, cols[hh]] = (_rms(o[hh], gain) * (zz * jax.nn.sigmoid(zz))).astype(o_ref.dtype)


def _dn_scan(proj3, conv_w, gb3, gbt, gain, *, q_col, k_col, v_col, z_col):
    b, t, _ = proj3.shape
    blk, nh = DN_BLOCK, DN_HEADS_PER_STEP
    nblk = t // blk
    width = nh * HEAD_DIM
    tok = lambda col: pl.BlockSpec((None, blk, width), lambda bi, g, i: (bi, i, col // nh + g))
    cw = lambda col: pl.BlockSpec((SHORT_CONV, width), lambda bi, g, i: (0, col // nh + g))
    return pl.pallas_call(
        functools.partial(_dn_kernel, blk=blk, nh=nh),
        grid=(b, DN_HEADS // nh, nblk),
        in_specs=[
            tok(q_col), tok(k_col), tok(v_col), tok(z_col),
            cw(0), cw(DN_HEADS), cw(2 * DN_HEADS),
            pl.BlockSpec((None, blk, 128), lambda bi, g, i: (bi, i, 0)),
            pl.BlockSpec((128, blk), lambda bi, g, i: (0, bi * nblk + i)),
            pl.BlockSpec((1, HEAD_DIM), lambda bi, g, i: (0, 0)),
        ],
        out_specs=pl.BlockSpec((None, blk, width), lambda bi, g, i: (bi, i, g)),
        out_shape=jax.ShapeDtypeStruct((b, t, DN_WIDTH), BF16),
        scratch_shapes=[
            pltpu.VMEM((nh, HEAD_DIM, HEAD_DIM), F32),
            pltpu.VMEM((blk + SUBLANES, width), F32),
            pltpu.VMEM((blk + SUBLANES, width), F32),
            pltpu.VMEM((blk + SUBLANES, width), F32),
        ],
        compiler_params=pltpu.CompilerParams(
            dimension_semantics=("arbitrary", "arbitrary", "arbitrary"),
            vmem_limit_bytes=VMEM_LIMIT_BYTES),
        name="dn_scan",
    )(proj3, proj3, proj3, proj3, conv_w, conv_w, conv_w, gb3, gbt, gain)


def _out_proj_kernel(osb_ref, odn_ref, w1_ref, w2_ref, x_ref, gpost_ref, gpre_ref, h_ref, xn_ref):
    m = _dot(osb_ref[...], w1_ref[...]) + _dot(odn_ref[...], w2_ref[...])
    hres = x_ref[...] + _rms(m, gpost_ref[...])
    h_ref[...] = hres
    xn_ref[...] = _rms(hres, gpre_ref[...]).astype(xn_ref.dtype)


def _out_proj(o_sb, o_dn, w1, w2, x2d, g_post, g_pre):
    m, d = x2d.shape
    tm = OUT_TM
    row = lambda width: pl.BlockSpec((tm, width), lambda i: (i, 0))
    full = lambda a: pl.BlockSpec(a.shape, lambda i: (0, 0))
    return pl.pallas_call(
        _out_proj_kernel,
        grid=(m // tm,),
        in_specs=[row(SB_WIDTH), row(DN_WIDTH), full(w1), full(w2), row(d), full(g_post), full(g_pre)],
        out_specs=[row(d), row(d)],
        out_shape=[jax.ShapeDtypeStruct((m, d), F32), jax.ShapeDtypeStruct((m, d), BF16)],
        compiler_params=pltpu.CompilerParams(
            dimension_semantics=("arbitrary",), vmem_limit_bytes=VMEM_LIMIT_BYTES),
        name="out_proj",
    )(o_sb, o_dn, w1, w2, x2d, g_post, g_pre)


def _ffn_kernel(xn_ref, wg_ref, wv_ref, cwg_ref, cwv_ref, bg_ref, bv_ref, wd_ref, h_ref, gain_ref,
                y_ref, acc_ref, bufg_ref, bufv_ref, tailg_ref, tailv_ref, *, tm, nf, tiles_per_seq):
    i = pl.program_id(0)
    f = pl.program_id(1)
    halo = SUBLANES
    parts = range(FFN_ROW_PARTS)
    rows = tm // FFN_ROW_PARTS
    seq_start = (i % tiles_per_seq) == 0

    @pl.when(f == 0)
    def _():
        acc_ref[...] = jnp.zeros_like(acc_ref)

    ups = []
    for p in parts:
        xn = xn_ref[p * rows:(p + 1) * rows, :]
        ups.append((_dot(xn, wg_ref[...]), _dot(xn, wv_ref[...])))

    def conv(p, which, cw_ref, b_ref, buf_ref, tail_ref):
        up = ups[p][which]
        prev = ups[p - 1][which][rows - halo:rows, :] if p else jnp.where(seq_start, 0.0, tail_ref[f])
        buf_ref[p, 0:halo, :] = prev
        buf_ref[p, halo:halo + rows, :] = up
        cw = cw_ref[...]
        out = up * cw[2:3, :] + b_ref[...]
        for j in range(FFN_CONV - 1):
            out = out + buf_ref[p, pl.ds(halo - (FFN_CONV - 1) + j, rows), :] * cw[j:j + 1, :]
        return out

    for p in parts:
        gate = conv(p, 0, cwg_ref, bg_ref, bufg_ref, tailg_ref)
        val = conv(p, 1, cwv_ref, bv_ref, bufv_ref, tailv_ref)
        act = (jax.nn.gelu(gate, approximate=True) * val).astype(BF16)
        acc_ref[p * rows:(p + 1) * rows, :] += _dot(act, wd_ref[...])
    tailg_ref[f] = ups[-1][0][rows - halo:rows, :]
    tailv_ref[f] = ups[-1][1][rows - halo:rows, :]

    @pl.when(f == nf - 1)
    def _():
        y_ref[...] = h_ref[...] + _rms(acc_ref[...], gain_ref[...])


def _conv_ffn(xn2, w_up, conv_w, conv_b, w_down, hres, gain, *, seq_len):
    m, d = hres.shape
    d_ff = w_down.shape[0]
    tm, tf = FFN_TM, FFN_TF
    nf = d_ff // tf
    tok = lambda i, f: (i, 0)
    return pl.pallas_call(
        functools.partial(_ffn_kernel, tm=tm, nf=nf, tiles_per_seq=seq_len // tm),
        grid=(m // tm, nf),
        in_specs=[
            pl.BlockSpec((tm, d), tok),
            pl.BlockSpec((d, tf), lambda i, f: (0, f)),
            pl.BlockSpec((d, tf), lambda i, f: (0, nf + f)),
            pl.BlockSpec((FFN_CONV, tf), lambda i, f: (0, f)),
            pl.BlockSpec((FFN_CONV, tf), lambda i, f: (0, nf + f)),
            pl.BlockSpec((1, tf), lambda i, f: (0, f)),
            pl.BlockSpec((1, tf), lambda i, f: (0, nf + f)),
            pl.BlockSpec((tf, d), lambda i, f: (f, 0)),
            pl.BlockSpec((tm, d), tok),
            pl.BlockSpec((1, d), lambda i, f: (0, 0)),
        ],
        out_specs=pl.BlockSpec((tm, d), tok),
        out_shape=jax.ShapeDtypeStruct((m, d), F32),
        scratch_shapes=[
            pltpu.VMEM((tm, d), F32),
            pltpu.VMEM((FFN_ROW_PARTS, tm // FFN_ROW_PARTS + SUBLANES, tf), F32),
            pltpu.VMEM((FFN_ROW_PARTS, tm // FFN_ROW_PARTS + SUBLANES, tf), F32),
            pltpu.VMEM((nf, SUBLANES, tf), F32),
            pltpu.VMEM((nf, SUBLANES, tf), F32),
        ],
        compiler_params=pltpu.CompilerParams(
            dimension_semantics=("arbitrary", "arbitrary"), vmem_limit_bytes=VMEM_LIMIT_BYTES),
        name="conv_ffn",
    )(xn2, w_up, w_up, conv_w, conv_w, conv_b, conv_b, w_down, hres, gain)


def _pad_lanes(vec):
    return jnp.pad(vec.astype(F32), (0, 128 - vec.shape[0]))[None, :]


def kernel(x, w_in, sb_out_gain, dn_conv_w, dn_a_log, dn_dt_bias, dn_out_gain, w_out, ln_mix_pre,
           ln_mix_post, w_up, ffn_conv_w, ffn_conv_b, w_down, ln_ffn_pre, ln_ffn_post):
    b, t, d = x.shape
    depth = w_in.shape[0]
    n_main = 3 * SB_WIDTH + 4 * DN_WIDTH
    assert t % DN_BLOCK == 0 and t % SB_TQ == 0 and t % FFN_TM == 0
    assert (b * t) % IN_TM == 0 and n_main % IN_TN == 0 and IN_TM % DN_BLOCK == 0
    hcols = lambda off: off // HEAD_DIM

    h2d = x.reshape(b * t, d)
    for l in range(depth):
        w_main = w_in[l, :, :n_main].astype(BF16)
        w_ba = jnp.pad(w_in[l, :, n_main:], ((0, 0), (0, 128 - 2 * DN_HEADS))).astype(BF16)
        alog = _pad_lanes(jnp.concatenate([jnp.zeros((DN_HEADS,), F32), dn_a_log[l]]))
        dtb = _pad_lanes(jnp.concatenate([jnp.zeros((DN_HEADS,), F32), dn_dt_bias[l]]))
        proj, gb, gbt = _in_proj(h2d, ln_mix_pre[l][None, :], w_main, w_ba, alog, dtb)
        proj3 = proj.reshape(b, t, n_main)

        o_sb = _sb_attention(proj3, sb_out_gain[l][None, :],
                             q_col=0, k_col=hcols(SB_WIDTH), v_col=hcols(2 * SB_WIDTH))
        dn0 = 3 * SB_WIDTH
        o_dn = _dn_scan(proj3, dn_conv_w[l], gb.reshape(b, t, 128), gbt, dn_out_gain[l][None, :],
                        q_col=hcols(dn0), k_col=hcols(dn0 + DN_WIDTH),
                        v_col=hcols(dn0 + 2 * DN_WIDTH), z_col=hcols(dn0 + 3 * DN_WIDTH))

        w_o = w_out[l].astype(BF16)
        hres, xn2 = _out_proj(o_sb.reshape(b * t, SB_WIDTH), o_dn.reshape(b * t, DN_WIDTH),
                              w_o[:SB_WIDTH], w_o[SB_WIDTH:], h2d,
                              ln_mix_post[l][None, :], ln_ffn_pre[l][None, :])

        h2d = _conv_ffn(xn2, w_up[l].astype(BF16), ffn_conv_w[l], ffn_conv_b[l][None, :],
                        w_down[l].astype(BF16), hres, ln_ffn_post[l][None, :], seq_len=t)
    return h2d.reshape(b, t, d)
```

```python
import functools

import jax
import jax.numpy as jnp
from jax import lax
from jax.experimental import pallas as pl
from jax.experimental.pallas import tpu as pltpu

F32 = jnp.float32
BF16 = jnp.bfloat16

HEAD_DIM = 128
SB_HEADS = 8
DN_HEADS = 8
SB_WIDTH = SB_HEADS * HEAD_DIM
DN_WIDTH = DN_HEADS * HEAD_DIM
SHORT_CONV = 4
FFN_CONV = 3
EPS = 1e-6

VMEM_LIMIT_BYTES = 56 * 1024 * 1024
SUBLANES = 8

IN_TM, IN_TN = 1024, 1792
SB_TQ = 256
SB_HEADS_PER_STEP = 8
LOG2E = 1.4426950408889634
DN_BLOCK = 256
DN_HEADS_PER_STEP = 8
OUT_TM = 512
FFN_TM, FFN_TF = 1024, 512
FFN_ROW_PARTS = 2


def _rms(x, gain):
    return x * lax.rsqrt(jnp.mean(x * x, axis=-1, keepdims=True) + EPS) * gain


def _dot(a, b):
    return jnp.dot(a, b, preferred_element_type=F32)


def _dot_nt(a, b):
    return lax.dot_general(a, b, (((1,), (1,)), ((), ())), preferred_element_type=F32)


def _dot_tn(a, b):
    return lax.dot_general(a, b, (((0,), (0,)), ((), ())), preferred_element_type=F32)


def _in_proj_kernel(x_ref, gain_ref, w_ref, cscale_ref, wba_ref, alog_ref, dtb_ref,
                    proj_ref, gb_ref, gbt_ref, xn_ref, *, tm):
    j = pl.program_id(1)

    @pl.when(j == 0)
    def _():
        xn = _rms(x_ref[...], gain_ref[...]).astype(BF16)
        xn_ref[...] = xn
        ba = _dot(xn, wba_ref[...])
        lane = lax.broadcasted_iota(jnp.int32, ba.shape, 1)
        beta = jax.nn.sigmoid(ba)
        g = -(jnp.exp(alog_ref[...]) * jax.nn.softplus(ba + dtb_ref[...]))
        r = lax.broadcasted_iota(jnp.int32, (DN_BLOCK, DN_BLOCK), 0)
        c = lax.broadcasted_iota(jnp.int32, (DN_BLOCK, DN_BLOCK), 1)
        tri = (r >= c).astype(BF16)
        g1 = g.astype(BF16)
        rem = g - g1.astype(F32)
        g2 = rem.astype(BF16)
        g3 = (rem - g2.astype(F32)).astype(BF16)
        gsplit = jnp.concatenate([g1, g2, g3], axis=1)
        gc = []
        for blk0 in range(0, tm, DN_BLOCK):
            part = _dot(tri, gsplit[blk0:blk0 + DN_BLOCK, :])
            gc.append(part[:, 0:128] + part[:, 128:256] + part[:, 256:384])
        gb = jnp.where(lane < DN_HEADS, beta, jnp.concatenate(gc, axis=0))
        gb_ref[...] = gb
        gbt_ref[...] = gb.T

    proj_ref[...] = (_dot(xn_ref[...], w_ref[...]) * cscale_ref[...]).astype(proj_ref.dtype)


def _in_proj(x2d, gain, w_main, w_ba, alog, dtb):
    m, d = x2d.shape
    n = w_main.shape[1]
    tm, tn = IN_TM, IN_TN
    cscale = jnp.where(jnp.arange(n) < SB_WIDTH, HEAD_DIM ** -0.5, 1.0).astype(F32)[None, :]
    return pl.pallas_call(
        functools.partial(_in_proj_kernel, tm=tm),
        grid=(m // tm, n // tn),
        in_specs=[
            pl.BlockSpec((tm, d), lambda i, j: (i, 0)),
            pl.BlockSpec((1, d), lambda i, j: (0, 0)),
            pl.BlockSpec((d, tn), lambda i, j: (0, j)),
            pl.BlockSpec((1, tn), lambda i, j: (0, j)),
            pl.BlockSpec((d, 128), lambda i, j: (0, 0)),
            pl.BlockSpec((1, 128), lambda i, j: (0, 0)),
            pl.BlockSpec((1, 128), lambda i, j: (0, 0)),
        ],
        out_specs=[
            pl.BlockSpec((tm, tn), lambda i, j: (i, j)),
            pl.BlockSpec((tm, 128), lambda i, j: (i, 0)),
            pl.BlockSpec((128, tm), lambda i, j: (0, i)),
        ],
        out_shape=[
            jax.ShapeDtypeStruct((m, n), BF16),
            jax.ShapeDtypeStruct((m, 128), F32),
            jax.ShapeDtypeStruct((128, m), F32),
        ],
        scratch_shapes=[pltpu.VMEM((tm, d), BF16)],
        compiler_params=pltpu.CompilerParams(
            dimension_semantics=("arbitrary", "arbitrary"), vmem_limit_bytes=VMEM_LIMIT_BYTES),
        name="in_proj",
    )(x2d, gain, w_main, cscale, w_ba, alog, dtb)


def _sb_kernel(q_ref, k_ref, v_ref, gain_ref, o_ref, acc_ref, run_ref, *, tq, nh):
    qi = pl.program_id(2)
    r = lax.broadcasted_iota(jnp.int32, (tq, tq), 0)
    c = lax.broadcasted_iota(jnp.int32, (tq, tq), 1)
    suffix = (r >= c).astype(BF16)
    half = tq // 2

    heads = range(nh)
    cols = [slice(hh * HEAD_DIM, (hh + 1) * HEAD_DIM) for hh in heads]

    def tiles(kb, diag):
        rows = pl.ds(kb * tq, tq)
        zs = [_dot_nt(q_ref[:, cols[hh]], k_ref[rows, cols[hh]]) for hh in heads]
        css = []
        for z in zs:
            sp = jnp.maximum(z, 0.0) + jnp.log(1.0 + jnp.exp2(jnp.abs(z) * (-LOG2E)))
            if diag:
                sp = jnp.where(c < r, sp, 0.0)
            css.append(_dot(sp.astype(BF16), suffix))
        pvs, totals = [], []
        for hh in heads:
            z, cs = zs[hh], css[hh]
            if diag:
                att = jnp.exp(jnp.where(c < r, z - cs, -jnp.inf).astype(BF16))
            else:
                run = run_ref[hh]
                att = jnp.concatenate(
                    [jnp.exp((z[:, :half] - cs[:, :half] - run).astype(BF16)),
                     jnp.exp((z[:, half:] - cs[:, half:] - run).astype(BF16))], axis=1)
            pvs.append(_dot(att, v_ref[rows, cols[hh]]))
            totals.append(jnp.broadcast_to(cs[:, 0:1], (tq, HEAD_DIM)))
        for hh in heads:
            if diag:
                acc_ref[hh] = pvs[hh]
                run_ref[hh] = totals[hh]
            else:
                acc_ref[hh] += pvs[hh]
                run_ref[hh] += totals[hh]

    tiles(qi, True)

    def body(step, carry):
        tiles(qi - 1 - step, False)
        return carry

    lax.fori_loop(0, qi, body, 0)
    gain = gain_ref[...]
    for hh in range(nh):
        o_ref[:, hh * HEAD_DIM:(hh + 1) * HEAD_DIM] = _rms(acc_ref[hh], gain).astype(o_ref.dtype)


def _sb_attention(proj3, gain, *, q_col, k_col, v_col):
    b, t, _ = proj3.shape
    tq, nh = SB_TQ, SB_HEADS_PER_STEP
    assert tq == 2 * HEAD_DIM
    width = nh * HEAD_DIM
    return pl.pallas_call(
        functools.partial(_sb_kernel, tq=tq, nh=nh),
        grid=(b, SB_HEADS // nh, t // tq),
        in_specs=[
            pl.BlockSpec((None, tq, width), lambda bi, g, i: (bi, i, q_col // nh + g)),
            pl.BlockSpec((None, t, width), lambda bi, g, i: (bi, 0, k_col // nh + g)),
            pl.BlockSpec((None, t, width), lambda bi, g, i: (bi, 0, v_col // nh + g)),
            pl.BlockSpec((1, HEAD_DIM), lambda bi, g, i: (0, 0)),
        ],
        out_specs=pl.BlockSpec((None, tq, width), lambda bi, g, i: (bi, i, g)),
        out_shape=jax.ShapeDtypeStruct((b, t, SB_WIDTH), BF16),
        scratch_shapes=[
            pltpu.VMEM((nh, tq, HEAD_DIM), F32),
            pltpu.VMEM((nh, tq, HEAD_DIM), F32),
        ],
        compiler_params=pltpu.CompilerParams(
            dimension_semantics=("arbitrary", "arbitrary", "arbitrary"),
            vmem_limit_bytes=VMEM_LIMIT_BYTES),
        name="sb_attn",
    )(proj3, proj3, proj3, gain)


def _dn_kernel(q_ref, k_ref, v_ref, z_ref, wq_ref, wk_ref, wv_ref, gb_ref, gbt_ref, gain_ref,
               o_ref, s_ref, bq_ref, bk_ref, bv_ref, *, blk, nh):
    grp = pl.program_id(1)
    i = pl.program_id(2)
    halo = SUBLANES
    heads = range(nh)
    cols = [slice(hh * HEAD_DIM, (hh + 1) * HEAD_DIM) for hh in heads]

    @pl.when(i == 0)
    def _():
        s_ref[...] = jnp.zeros_like(s_ref)
        for buf in (bq_ref, bk_ref, bv_ref):
            buf[0:halo, :] = jnp.zeros((halo, nh * HEAD_DIM), F32)

    def conv_silu(x_ref, w_ref, buf_ref):
        x = x_ref[...].astype(F32)
        buf_ref[halo:halo + blk, :] = x
        w = w_ref[...]
        y = x * w[3:4, :]
        for j in range(SHORT_CONV - 1):
            y = y + buf_ref[pl.ds(halo - (SHORT_CONV - 1) + j, blk), :] * w[j:j + 1, :]
        buf_ref[0:halo, :] = x[blk - halo:blk, :]
        return y * jax.nn.sigmoid(y)

    def l2n(x):
        return x * lax.rsqrt(jnp.sum(x * x, axis=-1, keepdims=True) + EPS)

    qc = conv_silu(q_ref, wq_ref, bq_ref)
    kc = conv_silu(k_ref, wk_ref, bk_ref)
    vc = conv_silu(v_ref, wv_ref, bv_ref)
    q = [l2n(qc[:, cols[hh]]) * (HEAD_DIM ** -0.5) for hh in heads]
    k = [l2n(kc[:, cols[hh]]) for hh in heads]
    v = [vc[:, cols[hh]] for hh in heads]

    gb = gb_ref[...]
    gbt = gbt_ref[...]
    lane = lax.broadcasted_iota(jnp.int32, gb.shape, 1)
    sub = lax.broadcasted_iota(jnp.int32, gbt.shape, 0)
    r = lax.broadcasted_iota(jnp.int32, (blk, blk), 0)
    c = lax.broadcasted_iota(jnp.int32, (blk, blk), 1)
    causal = r >= c

    beta, g_col, g_last, decay = [], [], [], []
    for hh in heads:
        head = grp * nh + hh
        beta.append(jnp.sum(jnp.where(lane == head, gb, 0.0), axis=1, keepdims=True))
        g_col.append(jnp.sum(jnp.where(lane == head + DN_HEADS, gb, 0.0), axis=1, keepdims=True))
        g_row = jnp.sum(jnp.where(sub == head + DN_HEADS, gbt, 0.0), axis=0, keepdims=True)
        g_last.append(g_row[:, blk - 1:blk])
        decay.append(jnp.where(causal, jnp.exp(jnp.where(causal, g_col[hh] - g_row, 0.0)), 0.0))

    k_b = [k[hh].astype(BF16) for hh in heads]
    kb = [k[hh] * beta[hh] for hh in heads]
    kk = [_dot_nt(kb[hh].astype(BF16), k_b[hh]) for hh in heads]
    qk = [_dot_nt(q[hh].astype(BF16), k_b[hh]) for hh in heads]
    a = [jnp.where(r > c, kk[hh] * decay[hh], 0.0) for hh in heads]
    qa = [(qk[hh] * decay[hh]).astype(BF16) for hh in heads]

    rc = r ^ c
    level = 31 - lax.clz(rc)
    eye = jnp.where(r == c, 1.0, 0.0)
    t = [eye - jnp.where(level == 0, a[hh], 0.0) for hh in heads]
    for p in range(1, blk.bit_length() - 1):
        t_b = [t[hh].astype(BF16) for hh in heads]
        cross = level == p
        prod = [_dot(jnp.where(cross, a[hh], 0.0).astype(BF16), t_b[hh]) for hh in heads]
        t = [t[hh] - _dot(t_b[hh], prod[hh].astype(BF16)) for hh in heads]

    eg = [jnp.exp(g_col[hh]) for hh in heads]
    t_b = [t[hh].astype(BF16) for hh in heads]
    u = [_dot(t_b[hh], (v[hh] * beta[hh]).astype(BF16)) for hh in heads]
    w = [_dot(t_b[hh], (kb[hh] * eg[hh]).astype(BF16)) for hh in heads]
    qd = [(q[hh] * eg[hh]).astype(BF16) for hh in heads]
    kt = [(k[hh] * jnp.exp(g_last[hh] - g_col[hh])).astype(BF16) for hh in heads]

    s = [s_ref[hh] for hh in heads]
    s_b = [s[hh].astype(BF16) for hh in heads]
    ws = [_dot(w[hh].astype(BF16), s_b[hh]) for hh in heads]
    qs = [_dot(qd[hh], s_b[hh]) for hh in heads]
    vn_b = [(u[hh] - ws[hh]).astype(BF16) for hh in heads]
    o = [qs[hh] + _dot(qa[hh], vn_b[hh]) for hh in heads]
    for hh in heads:
        s_ref[hh] = s[hh] * jnp.exp(g_last[hh]) + _dot_tn(kt[hh], vn_b[hh])

    gain = gain_ref[...]
    for hh in heads:
        zz = z_ref[:, cols[hh]].astype(F32)
        o_ref[:, cols[hh]] = (_rms(o[hh], gain) * (zz * jax.nn.sigmoid(zz))).astype(o_ref.dtype)


def _dn_scan(proj3, conv_w, gb3, gbt, gain, *, q_col, k_col, v_col, z_col):
    b, t, _ = proj3.shape
    blk, nh = DN_BLOCK, DN_HEADS_PER_STEP
    nblk = t // blk
    width = nh * HEAD_DIM
    tok = lambda col: pl.BlockSpec((None, blk, width), lambda bi, g, i: (bi, i, col // nh + g))
    cw = lambda col: pl.BlockSpec((SHORT_CONV, width), lambda bi, g, i: (0, col // nh + g))
    return pl.pallas_call(
        functools.partial(_dn_kernel, blk=blk, nh=nh),
        grid=(b, DN_HEADS // nh, nblk),
        in_specs=[
            tok(q_col), tok(k_col), tok(v_col), tok(z_col),
            cw(0), cw(DN_HEADS), cw(2 * DN_HEADS),
            pl.BlockSpec((None, blk, 128), lambda bi, g, i: (bi, i, 0)),
            pl.BlockSpec((128, blk), lambda bi, g, i: (0, bi * nblk + i)),
            pl.BlockSpec((1, HEAD_DIM), lambda bi, g, i: (0, 0)),
        ],
        out_specs=pl.BlockSpec((None, blk, width), lambda bi, g, i: (bi, i, g)),
        out_shape=jax.ShapeDtypeStruct((b, t, DN_WIDTH), BF16),
        scratch_shapes=[
            pltpu.VMEM((nh, HEAD_DIM, HEAD_DIM), F32),
            pltpu.VMEM((blk + SUBLANES, width), F32),
            pltpu.VMEM((blk + SUBLANES, width), F32),
            pltpu.VMEM((blk + SUBLANES, width), F32),
        ],
        compiler_params=pltpu.CompilerParams(
            dimension_semantics=("arbitrary", "arbitrary", "arbitrary"),
            vmem_limit_bytes=VMEM_LIMIT_BYTES),
        name="dn_scan",
    )(proj3, proj3, proj3, proj3, conv_w, conv_w, conv_w, gb3, gbt, gain)


def _out_proj_kernel(osb_ref, odn_ref, w1_ref, w2_ref, x_ref, gpost_ref, gpre_ref, h_ref, xn_ref):
    m = _dot(osb_ref[...], w1_ref[...]) + _dot(odn_ref[...], w2_ref[...])
    hres = x_ref[...] + _rms(m, gpost_ref[...])
    h_ref[...] = hres
    xn_ref[...] = _rms(hres, gpre_ref[...]).astype(xn_ref.dtype)


def _out_proj(o_sb, o_dn, w1, w2, x2d, g_post, g_pre):
    m, d = x2d.shape
    tm = OUT_TM
    row = lambda width: pl.BlockSpec((tm, width), lambda i: (i, 0))
    full = lambda a: pl.BlockSpec(a.shape, lambda i: (0, 0))
    return pl.pallas_call(
        _out_proj_kernel,
        grid=(m // tm,),
        in_specs=[row(SB_WIDTH), row(DN_WIDTH), full(w1), full(w2), row(d), full(g_post), full(g_pre)],
        out_specs=[row(d), row(d)],
        out_shape=[jax.ShapeDtypeStruct((m, d), F32), jax.ShapeDtypeStruct((m, d), BF16)],
        compiler_params=pltpu.CompilerParams(
            dimension_semantics=("arbitrary",), vmem_limit_bytes=VMEM_LIMIT_BYTES),
        name="out_proj",
    )(o_sb, o_dn, w1, w2, x2d, g_post, g_pre)


def _ffn_kernel(xn_ref, wg_ref, wv_ref, cwg_ref, cwv_ref, bg_ref, bv_ref, wd_ref, gain_ref, h_hbm,
                y_hbm, acc_ref, bufg_ref, bufv_ref, tailg_ref, tailv_ref, hy_ref, h_sem, y_sem,
                *, tm, nf, tiles_per_seq):
    i = pl.program_id(0)
    f = pl.program_id(1)
    halo = SUBLANES
    parts = range(FFN_ROW_PARTS)
    rows = tm // FFN_ROW_PARTS
    seq_start = (i % tiles_per_seq) == 0

    def h_copy(tile):
        return pltpu.make_async_copy(h_hbm.at[pl.ds(tile * tm, tm), :], hy_ref, h_sem)

    def y_copy(tile):
        return pltpu.make_async_copy(hy_ref, y_hbm.at[pl.ds(tile * tm, tm), :], y_sem)

    @pl.when(f == 0)
    def _():
        acc_ref[...] = jnp.zeros_like(acc_ref)

    @pl.when(f == 1)
    def _():
        @pl.when(i > 0)
        def _():
            y_copy(i - 1).wait()
        h_copy(i).start()

    ups = []
    for p in parts:
        xn = xn_ref[p * rows:(p + 1) * rows, :]
        ups.append((_dot(xn, wg_ref[...]), _dot(xn, wv_ref[...])))

    def conv(p, which, cw_ref, b_ref, buf_ref, tail_ref):
        up = ups[p][which]
        prev = ups[p - 1][which][rows - halo:rows, :] if p else jnp.where(seq_start, 0.0, tail_ref[f])
        buf_ref[p, 0:halo, :] = prev
        buf_ref[p, halo:halo + rows, :] = up
        cw = cw_ref[...]
        out = up * cw[2:3, :] + b_ref[...]
        for j in range(FFN_CONV - 1):
            out = out + buf_ref[p, pl.ds(halo - (FFN_CONV - 1) + j, rows), :] * cw[j:j + 1, :]
        return out

    for p in parts:
        gate = conv(p, 0, cwg_ref, bg_ref, bufg_ref, tailg_ref)
        val = conv(p, 1, cwv_ref, bv_ref, bufv_ref, tailv_ref)
        act = (jax.nn.gelu(gate, approximate=True) * val).astype(BF16)
        acc_ref[p * rows:(p + 1) * rows, :] += _dot(act, wd_ref[...])
    tailg_ref[f] = ups[-1][0][rows - halo:rows, :]
    tailv_ref[f] = ups[-1][1][rows - halo:rows, :]

    @pl.when(f == nf - 1)
    def _():
        h_copy(i).wait()
        hy_ref[...] = hy_ref[...] + _rms(acc_ref[...], gain_ref[...])
        y_copy(i).start()

        @pl.when(i == pl.num_programs(0) - 1)
        def _():
            y_copy(i).wait()


def _conv_ffn(xn2, w_up, conv_w, conv_b, w_down, hres, gain, *, seq_len):
    m, d = hres.shape
    d_ff = w_down.shape[0]
    tm, tf = FFN_TM, FFN_TF
    nf = d_ff // tf
    assert nf >= 3
    tok = lambda i, f: (i, 0)
    return pl.pallas_call(
        functools.partial(_ffn_kernel, tm=tm, nf=nf, tiles_per_seq=seq_len // tm),
        grid=(m // tm, nf),
        in_specs=[
            pl.BlockSpec((tm, d), tok),
            pl.BlockSpec((d, tf), lambda i, f: (0, f)),
            pl.BlockSpec((d, tf), lambda i, f: (0, nf + f)),
            pl.BlockSpec((FFN_CONV, tf), lambda i, f: (0, f)),
            pl.BlockSpec((FFN_CONV, tf), lambda i, f: (0, nf + f)),
            pl.BlockSpec((1, tf), lambda i, f: (0, f)),
            pl.BlockSpec((1, tf), lambda i, f: (0, nf + f)),
            pl.BlockSpec((tf, d), lambda i, f: (f, 0)),
            pl.BlockSpec((1, d), lambda i, f: (0, 0)),
            pl.BlockSpec(memory_space=pl.ANY),
        ],
        out_specs=pl.BlockSpec(memory_space=pl.ANY),
        out_shape=jax.ShapeDtypeStruct((m, d), F32),
        scratch_shapes=[
            pltpu.VMEM((tm, d), F32),
            pltpu.VMEM((FFN_ROW_PARTS, tm // FFN_ROW_PARTS + SUBLANES, tf), F32),
            pltpu.VMEM((FFN_ROW_PARTS, tm // FFN_ROW_PARTS + SUBLANES, tf), F32),
            pltpu.VMEM((nf, SUBLANES, tf), F32),
            pltpu.VMEM((nf, SUBLANES, tf), F32),
            pltpu.VMEM((tm, d), F32),
            pltpu.SemaphoreType.DMA(()),
            pltpu.SemaphoreType.DMA(()),
        ],
        compiler_params=pltpu.CompilerParams(
            dimension_semantics=("arbitrary", "arbitrary"), vmem_limit_bytes=VMEM_LIMIT_BYTES),
        name="conv_ffn",
    )(xn2, w_up, w_up, conv_w, conv_w, conv_b, conv_b, w_down, gain, hres)


def _pad_lanes(vec):
    return jnp.pad(vec.astype(F32), (0, 128 - vec.shape[0]))[None, :]


def kernel(x, w_in, sb_out_gain, dn_conv_w, dn_a_log, dn_dt_bias, dn_out_gain, w_out, ln_mix_pre,
           ln_mix_post, w_up, ffn_conv_w, ffn_conv_b, w_down, ln_ffn_pre, ln_ffn_post):
    b, t, d = x.shape
    depth = w_in.shape[0]
    n_main = 3 * SB_WIDTH + 4 * DN_WIDTH
    assert t % DN_BLOCK == 0 and t % SB_TQ == 0 and t % FFN_TM == 0
    assert (b * t) % IN_TM == 0 and n_main % IN_TN == 0 and IN_TM % DN_BLOCK == 0
    hcols = lambda off: off // HEAD_DIM

    h2d = x.reshape(b * t, d)
    for l in range(depth):
        w_main = w_in[l, :, :n_main].astype(BF16)
        w_ba = jnp.pad(w_in[l, :, n_main:], ((0, 0), (0, 128 - 2 * DN_HEADS))).astype(BF16)
        alog = _pad_lanes(jnp.concatenate([jnp.zeros((DN_HEADS,), F32), dn_a_log[l]]))
        dtb = _pad_lanes(jnp.concatenate([jnp.zeros((DN_HEADS,), F32), dn_dt_bias[l]]))
        proj, gb, gbt = _in_proj(h2d, ln_mix_pre[l][None, :], w_main, w_ba, alog, dtb)
        proj3 = proj.reshape(b, t, n_main)

        o_sb = _sb_attention(proj3, sb_out_gain[l][None, :],
                             q_col=0, k_col=hcols(SB_WIDTH), v_col=hcols(2 * SB_WIDTH))
        dn0 = 3 * SB_WIDTH
        o_dn = _dn_scan(proj3, dn_conv_w[l], gb.reshape(b, t, 128), gbt, dn_out_gain[l][None, :],
                        q_col=hcols(dn0), k_col=hcols(dn0 + DN_WIDTH),
                        v_col=hcols(dn0 + 2 * DN_WIDTH), z_col=hcols(dn0 + 3 * DN_WIDTH))

        w_o = w_out[l].astype(BF16)
        hres, xn2 = _out_proj(o_sb.reshape(b * t, SB_WIDTH), o_dn.reshape(b * t, DN_WIDTH),
                              w_o[:SB_WIDTH], w_o[SB_WIDTH:], h2d,
                              ln_mix_post[l][None, :], ln_ffn_pre[l][None, :])

        h2d = _conv_ffn(xn2, w_up[l].astype(BF16), ffn_conv_w[l], ffn_conv_b[l][None, :],
                        w_down[l].astype(BF16), hres, ln_ffn_post[l][None, :], seq_len=t)
    return h2d.reshape(b, t, d)
```

```python
import functools

import jax
import jax.numpy as jnp
from jax import lax
from jax.experimental import pallas as pl
from jax.experimental.pallas import tpu as pltpu

F32 = jnp.float32
BF16 = jnp.bfloat16

HEAD_DIM = 128
SB_HEADS = 8
DN_HEADS = 8
SB_WIDTH = SB_HEADS * HEAD_DIM
DN_WIDTH = DN_HEADS * HEAD_DIM
SHORT_CONV = 4
FFN_CONV = 3
EPS = 1e-6

VMEM_LIMIT_BYTES = 56 * 1024 * 1024
SUBLANES = 8

IN_TM, IN_TN = 1024, 1792
SB_TQ, SB_TK = 512, 256
SB_HEADS_PER_STEP = 8
LOG2E = 1.4426950408889634
DN_BLOCK = 256
DN_HEADS_PER_STEP = 8
OUT_TM = 512
OUT_ROW_PARTS = 2
FFN_TM, FFN_TF = 1024, 512
FFN_ROW_PARTS = 4


def _rms(x, gain):
    return x * lax.rsqrt(jnp.mean(x * x, axis=-1, keepdims=True) + EPS) * gain


def _dot(a, b):
    return jnp.dot(a, b, preferred_element_type=F32)


def _dot_nt(a, b):
    return lax.dot_general(a, b, (((1,), (1,)), ((), ())), preferred_element_type=F32)


def _dot_tn(a, b):
    return lax.dot_general(a, b, (((0,), (0,)), ((), ())), preferred_element_type=F32)


def _in_proj_kernel(x_ref, gain_ref, w_ref, cscale_ref, wba_ref, alog_ref, dtb_ref,
                    proj_ref, gb_ref, gbt_ref, xn_ref, *, tm):
    j = pl.program_id(1)

    @pl.when(j == 0)
    def _():
        xn = _rms(x_ref[...], gain_ref[...]).astype(BF16)
        xn_ref[...] = xn
        ba = _dot(xn, wba_ref[...])
        lane = lax.broadcasted_iota(jnp.int32, ba.shape, 1)
        beta = jax.nn.sigmoid(ba)
        g = -(jnp.exp(alog_ref[...]) * jax.nn.softplus(ba + dtb_ref[...]))
        r = lax.broadcasted_iota(jnp.int32, (DN_BLOCK, DN_BLOCK), 0)
        c = lax.broadcasted_iota(jnp.int32, (DN_BLOCK, DN_BLOCK), 1)
        tri = (r >= c).astype(BF16)
        g1 = g.astype(BF16)
        rem = g - g1.astype(F32)
        g2 = rem.astype(BF16)
        g3 = (rem - g2.astype(F32)).astype(BF16)
        gsplit = jnp.concatenate([g1, g2, g3], axis=1)
        gc = []
        for blk0 in range(0, tm, DN_BLOCK):
            part = _dot(tri, gsplit[blk0:blk0 + DN_BLOCK, :])
            gc.append(part[:, 0:128] + part[:, 128:256] + part[:, 256:384])
        gb = jnp.where(lane < DN_HEADS, beta, jnp.concatenate(gc, axis=0))
        gb_ref[...] = gb
        gbt_ref[...] = gb.T

    proj_ref[...] = (_dot(xn_ref[...], w_ref[...]) * cscale_ref[...]).astype(proj_ref.dtype)


def _in_proj(x2d, gain, w_main, w_ba, alog, dtb):
    m, d = x2d.shape
    n = w_main.shape[1]
    tm, tn = IN_TM, IN_TN
    cscale = jnp.where(jnp.arange(n) < SB_WIDTH, HEAD_DIM ** -0.5, 1.0).astype(F32)[None, :]
    return pl.pallas_call(
        functools.partial(_in_proj_kernel, tm=tm),
        grid=(m // tm, n // tn),
        in_specs=[
            pl.BlockSpec((tm, d), lambda i, j: (i, 0)),
            pl.BlockSpec((1, d), lambda i, j: (0, 0)),
            pl.BlockSpec((d, tn), lambda i, j: (0, j)),
            pl.BlockSpec((1, tn), lambda i, j: (0, j)),
            pl.BlockSpec((d, 128), lambda i, j: (0, 0)),
            pl.BlockSpec((1, 128), lambda i, j: (0, 0)),
            pl.BlockSpec((1, 128), lambda i, j: (0, 0)),
        ],
        out_specs=[
            pl.BlockSpec((tm, tn), lambda i, j: (i, j)),
            pl.BlockSpec((tm, 128), lambda i, j: (i, 0)),
            pl.BlockSpec((128, tm), lambda i, j: (0, i)),
        ],
        out_shape=[
            jax.ShapeDtypeStruct((m, n), BF16),
            jax.ShapeDtypeStruct((m, 128), F32),
            jax.ShapeDtypeStruct((128, m), F32),
        ],
        scratch_shapes=[pltpu.VMEM((tm, d), BF16)],
        compiler_params=pltpu.CompilerParams(
            dimension_semantics=("arbitrary", "arbitrary"), vmem_limit_bytes=VMEM_LIMIT_BYTES),
        name="in_proj",
    )(x2d, gain, w_main, cscale, w_ba, alog, dtb)


def _sb_kernel(q_ref, k_ref, v_ref, gain_ref, o_ref, acc_ref, run_ref, *, tq, tk, nh):
    qi = pl.program_id(2)
    r = lax.broadcasted_iota(jnp.int32, (tk, tk), 0)
    c = lax.broadcasted_iota(jnp.int32, (tk, tk), 1)
    suffix = (r >= c).astype(BF16)
    half = tk // 2
    heads = range(nh)
    cols = [slice(hh * HEAD_DIM, (hh + 1) * HEAD_DIM) for hh in heads]

    acc_ref[...] = jnp.zeros_like(acc_ref)
    run_ref[...] = jnp.zeros_like(run_ref)

    def tiles(kb, row0, nrows, causal):
        keys = pl.ds(kb * tk, tk)
        qrows = slice(row0, row0 + nrows)
        if causal:
            rr = lax.broadcasted_iota(jnp.int32, (nrows, tk), 0)
            cc = lax.broadcasted_iota(jnp.int32, (nrows, tk), 1)
            valid = cc < rr
        zs = [_dot_nt(q_ref[qrows, cols[hh]], k_ref[keys, cols[hh]]) for hh in heads]
        css = []
        for z in zs:
            sp = jnp.maximum(z, 0.0) + jnp.log(1.0 + jnp.exp2(jnp.abs(z) * (-LOG2E)))
            if causal:
                sp = jnp.where(valid, sp, 0.0)
            css.append(_dot(sp.astype(BF16), suffix))
        pvs, totals = [], []
        for hh in heads:
            z, cs, run = zs[hh], css[hh], run_ref[hh, qrows, :]
            args = [z[:, :half] - cs[:, :half] - run, z[:, half:] - cs[:, half:] - run]
            if causal:
                args = [jnp.where(valid[:, :half], args[0], -jnp.inf),
                        jnp.where(valid[:, half:], args[1], -jnp.inf)]
            att = jnp.concatenate([jnp.exp(a.astype(BF16)) for a in args], axis=1)
            pvs.append(_dot(att, v_ref[keys, cols[hh]]))
            totals.append(jnp.broadcast_to(cs[:, 0:1], (nrows, HEAD_DIM)))
        for hh in heads:
            acc_ref[hh, qrows, :] += pvs[hh]
            run_ref[hh, qrows, :] += totals[hh]

    nk = tq // tk
    for j in reversed(range(nk)):
        tiles(qi * nk + j, j * tk, tq - j * tk, True)

    def body(step, carry):
        tiles(qi * nk - 1 - step, 0, tq, False)
        return carry

    lax.fori_loop(0, qi * nk, body, 0)
    gain = gain_ref[...]
    for hh in range(nh):
        o_ref[:, hh * HEAD_DIM:(hh + 1) * HEAD_DIM] = _rms(acc_ref[hh], gain).astype(o_ref.dtype)


def _sb_attention(proj3, gain, *, q_col, k_col, v_col):
    b, t, _ = proj3.shape
    tq, tk, nh = SB_TQ, SB_TK, SB_HEADS_PER_STEP
    assert tk == 2 * HEAD_DIM and tq % tk == 0
    width = nh * HEAD_DIM
    whole = lambda col: pl.BlockSpec((None, t, width), lambda bi, g, i: (bi, 0, col // nh + g),
                                     pipeline_mode=pl.Buffered(1))
    return pl.pallas_call(
        functools.partial(_sb_kernel, tq=tq, tk=tk, nh=nh),
        grid=(b, SB_HEADS // nh, t // tq),
        in_specs=[
            pl.BlockSpec((None, tq, width), lambda bi, g, i: (bi, i, q_col // nh + g)),
            whole(k_col),
            whole(v_col),
            pl.BlockSpec((1, HEAD_DIM), lambda bi, g, i: (0, 0)),
        ],
        out_specs=pl.BlockSpec((None, tq, width), lambda bi, g, i: (bi, i, g)),
        out_shape=jax.ShapeDtypeStruct((b, t, SB_WIDTH), BF16),
        scratch_shapes=[
            pltpu.VMEM((nh, tq, HEAD_DIM), F32),
            pltpu.VMEM((nh, tq, HEAD_DIM), F32),
        ],
        compiler_params=pltpu.CompilerParams(
            dimension_semantics=("arbitrary", "arbitrary", "arbitrary"),
            vmem_limit_bytes=VMEM_LIMIT_BYTES),
        name="sb_attn",
    )(proj3, proj3, proj3, gain)


def _dn_kernel(q_ref, k_ref, v_ref, z_ref, wq_ref, wk_ref, wv_ref, gb_ref, gbt_ref, gain_ref,
               o_ref, s_ref, bq_ref, bk_ref, bv_ref, *, blk, nh):
    grp = pl.program_id(1)
    i = pl.program_id(2)
    halo = SUBLANES
    heads = range(nh)
    cols = [slice(hh * HEAD_DIM, (hh + 1) * HEAD_DIM) for hh in heads]

    @pl.when(i == 0)
    def _():
        s_ref[...] = jnp.zeros_like(s_ref)
        for buf in (bq_ref, bk_ref, bv_ref):
            buf[0:halo, :] = jnp.zeros((halo, nh * HEAD_DIM), F32)

    def conv_silu(x_ref, w_ref, buf_ref):
        x = x_ref[...].astype(F32)
        buf_ref[halo:halo + blk, :] = x
        w = w_ref[...]
        y = x * w[3:4, :]
        for j in range(SHORT_CONV - 1):
            y = y + buf_ref[pl.ds(halo - (SHORT_CONV - 1) + j, blk), :] * w[j:j + 1, :]
        buf_ref[0:halo, :] = x[blk - halo:blk, :]
        return y * jax.nn.sigmoid(y)

    def l2n(x):
        return x * lax.rsqrt(jnp.sum(x * x, axis=-1, keepdims=True) + EPS)

    qc = conv_silu(q_ref, wq_ref, bq_ref)
    kc = conv_silu(k_ref, wk_ref, bk_ref)
    vc = conv_silu(v_ref, wv_ref, bv_ref)
    q = [l2n(qc[:, cols[hh]]) * (HEAD_DIM ** -0.5) for hh in heads]
    k = [l2n(kc[:, cols[hh]]) for hh in heads]
    v = [vc[:, cols[hh]] for hh in heads]

    gb = gb_ref[...]
    gbt = gbt_ref[...]
    lane = lax.broadcasted_iota(jnp.int32, gb.shape, 1)
    sub = lax.broadcasted_iota(jnp.int32, gbt.shape, 0)
    r = lax.broadcasted_iota(jnp.int32, (blk, blk), 0)
    c = lax.broadcasted_iota(jnp.int32, (blk, blk), 1)
    causal = r >= c

    beta, g_col, g_last, decay = [], [], [], []
    for hh in heads:
        head = grp * nh + hh
        beta.append(jnp.sum(jnp.where(lane == head, gb, 0.0), axis=1, keepdims=True))
        g_col.append(jnp.sum(jnp.where(lane == head + DN_HEADS, gb, 0.0), axis=1, keepdims=True))
        g_row = jnp.sum(jnp.where(sub == head + DN_HEADS, gbt, 0.0), axis=0, keepdims=True)
        g_last.append(g_row[:, blk - 1:blk])
        decay.append(jnp.where(causal, jnp.exp(jnp.where(causal, g_col[hh] - g_row, 0.0)), 0.0))

    k_b = [k[hh].astype(BF16) for hh in heads]
    kb = [k[hh] * beta[hh] for hh in heads]
    kk = [_dot_nt(kb[hh].astype(BF16), k_b[hh]) for hh in heads]
    qk = [_dot_nt(q[hh].astype(BF16), k_b[hh]) for hh in heads]
    a = [jnp.where(r > c, kk[hh] * decay[hh], 0.0) for hh in heads]
    qa = [(qk[hh] * decay[hh]).astype(BF16) for hh in heads]

    rc = r ^ c
    level = 31 - lax.clz(rc)
    eye = jnp.where(r == c, 1.0, 0.0)
    t = [eye - jnp.where(level == 0, a[hh], 0.0) for hh in heads]
    for p in range(1, blk.bit_length() - 1):
        t_b = [t[hh].astype(BF16) for hh in heads]
        cross = level == p
        prod = [_dot(jnp.where(cross, a[hh], 0.0).astype(BF16), t_b[hh]) for hh in heads]
        t = [t[hh] - _dot(t_b[hh], prod[hh].astype(BF16)) for hh in heads]

    eg = [jnp.exp(g_col[hh]) for hh in heads]
    t_b = [t[hh].astype(BF16) for hh in heads]
    u = [_dot(t_b[hh], (v[hh] * beta[hh]).astype(BF16)) for hh in heads]
    w = [_dot(t_b[hh], (kb[hh] * eg[hh]).astype(BF16)) for hh in heads]
    qd = [(q[hh] * eg[hh]).astype(BF16) for hh in heads]
    kt = [(k[hh] * jnp.exp(g_last[hh] - g_col[hh])).astype(BF16) for hh in heads]

    s = [s_ref[hh] for hh in heads]
    s_b = [s[hh].astype(BF16) for hh in heads]
    ws = [_dot(w[hh].astype(BF16), s_b[hh]) for hh in heads]
    qs = [_dot(qd[hh], s_b[hh]) for hh in heads]
    vn_b = [(u[hh] - ws[hh]).astype(BF16) for hh in heads]
    o = [qs[hh] + _dot(qa[hh], vn_b[hh]) for hh in heads]
    for hh in heads:
        s_ref[hh] = s[hh] * jnp.exp(g_last[hh]) + _dot_tn(kt[hh], vn_b[hh])

    gain = gain_ref[...]
    for hh in heads:
        zz = z_ref[:, cols[hh]].astype(F32)
        o_ref[:, cols[hh]] = (_rms(o[hh], gain) * (zz * jax.nn.sigmoid(zz))).astype(o_ref.dtype)


def _dn_scan(proj3, conv_w, gb3, gbt, gain, *, q_col, k_col, v_col, z_col):
    b, t, _ = proj3.shape
    blk, nh = DN_BLOCK, DN_HEADS_PER_STEP
    nblk = t // blk
    width = nh * HEAD_DIM
    tok = lambda col: pl.BlockSpec((None, blk, width), lambda bi, g, i: (bi, i, col // nh + g))
    cw = lambda col: pl.BlockSpec((SHORT_CONV, width), lambda bi, g, i: (0, col // nh + g))
    return pl.pallas_call(
        functools.partial(_dn_kernel, blk=blk, nh=nh),
        grid=(b, DN_HEADS // nh, nblk),
        in_specs=[
            tok(q_col), tok(k_col), tok(v_col), tok(z_col),
            cw(0), cw(DN_HEADS), cw(2 * DN_HEADS),
            pl.BlockSpec((None, blk, 128), lambda bi, g, i: (bi, i, 0)),
            pl.BlockSpec((128, blk), lambda bi, g, i: (0, bi * nblk + i)),
            pl.BlockSpec((1, HEAD_DIM), lambda bi, g, i: (0, 0)),
        ],
        out_specs=pl.BlockSpec((None, blk, width), lambda bi, g, i: (bi, i, g)),
        out_shape=jax.ShapeDtypeStruct((b, t, DN_WIDTH), BF16),
        scratch_shapes=[
            pltpu.VMEM((nh, HEAD_DIM, HEAD_DIM), F32),
            pltpu.VMEM((blk + SUBLANES, width), F32),
            pltpu.VMEM((blk + SUBLANES, width), F32),
            pltpu.VMEM((blk + SUBLANES, width), F32),
        ],
        compiler_params=pltpu.CompilerParams(
            dimension_semantics=("arbitrary", "arbitrary", "arbitrary"),
            vmem_limit_bytes=VMEM_LIMIT_BYTES),
        name="dn_scan",
    )(proj3, proj3, proj3, proj3, conv_w, conv_w, conv_w, gb3, gbt, gain)


def _out_proj_kernel(osb_ref, odn_ref, w1_ref, w2_ref, x_ref, gpost_ref, gpre_ref, h_ref, xn_ref):
    rows = osb_ref.shape[0] // OUT_ROW_PARTS
    for p in range(OUT_ROW_PARTS):
        rs = slice(p * rows, (p + 1) * rows)
        m = _dot(osb_ref[rs, :], w1_ref[...]) + _dot(odn_ref[rs, :], w2_ref[...])
        hres = x_ref[rs, :] + _rms(m, gpost_ref[...])
        h_ref[rs, :] = hres
        xn_ref[rs, :] = _rms(hres, gpre_ref[...]).astype(xn_ref.dtype)


def _out_proj(o_sb, o_dn, w1, w2, x2d, g_post, g_pre):
    m, d = x2d.shape
    tm = OUT_TM
    row = lambda width: pl.BlockSpec((tm, width), lambda i: (i, 0))
    full = lambda a: pl.BlockSpec(a.shape, lambda i: (0, 0))
    return pl.pallas_call(
        _out_proj_kernel,
        grid=(m // tm,),
        in_specs=[row(SB_WIDTH), row(DN_WIDTH), full(w1), full(w2), row(d), full(g_post), full(g_pre)],
        out_specs=[row(d), row(d)],
        out_shape=[jax.ShapeDtypeStruct((m, d), F32), jax.ShapeDtypeStruct((m, d), BF16)],
        compiler_params=pltpu.CompilerParams(
            dimension_semantics=("arbitrary",), vmem_limit_bytes=VMEM_LIMIT_BYTES),
        name="out_proj",
    )(o_sb, o_dn, w1, w2, x2d, g_post, g_pre)


def _ffn_kernel(xn_ref, wg_ref, wv_ref, cwg_ref, cwv_ref, bg_ref, bv_ref, wd_ref, gain_ref, h_hbm,
                y_hbm, acc_ref, bufg_ref, bufv_ref, tailg_ref, tailv_ref, hy_ref, h_sem, y_sem,
                *, tm, nf, tiles_per_seq):
    i = pl.program_id(0)
    f = pl.program_id(1)
    halo = SUBLANES
    parts = range(FFN_ROW_PARTS)
    rows = tm // FFN_ROW_PARTS
    seq_start = (i % tiles_per_seq) == 0

    def h_copy(tile):
        return pltpu.make_async_copy(h_hbm.at[pl.ds(tile * tm, tm), :], hy_ref, h_sem)

    def y_copy(tile):
        return pltpu.make_async_copy(hy_ref, y_hbm.at[pl.ds(tile * tm, tm), :], y_sem)

    @pl.when(f == 0)
    def _():
        acc_ref[...] = jnp.zeros_like(acc_ref)

    @pl.when(f == 1)
    def _():
        @pl.when(i > 0)
        def _():
            y_copy(i - 1).wait()
        h_copy(i).start()

    ups = []
    for p in parts:
        xn = xn_ref[p * rows:(p + 1) * rows, :]
        ups.append((_dot(xn, wg_ref[...]), _dot(xn, wv_ref[...])))

    def conv(p, which, cw_ref, b_ref, buf_ref, tail_ref):
        up = ups[p][which]
        prev = ups[p - 1][which][rows - halo:rows, :] if p else jnp.where(seq_start, 0.0, tail_ref[f])
        buf_ref[p, 0:halo, :] = prev
        buf_ref[p, halo:halo + rows, :] = up
        cw = cw_ref[...]
        out = up * cw[2:3, :] + b_ref[...]
        for j in range(FFN_CONV - 1):
            out = out + buf_ref[p, pl.ds(halo - (FFN_CONV - 1) + j, rows), :] * cw[j:j + 1, :]
        return out

    for p in parts:
        gate = conv(p, 0, cwg_ref, bg_ref, bufg_ref, tailg_ref)
        val = conv(p, 1, cwv_ref, bv_ref, bufv_ref, tailv_ref)
        act = (jax.nn.gelu(gate, approximate=True) * val).astype(BF16)
        acc_ref[p * rows:(p + 1) * rows, :] += _dot(act, wd_ref[...])
    tailg_ref[f] = ups[-1][0][rows - halo:rows, :]
    tailv_ref[f] = ups[-1][1][rows - halo:rows, :]

    @pl.when(f == nf - 1)
    def _():
        h_copy(i).wait()
        hy_ref[...] = hy_ref[...] + _rms(acc_ref[...], gain_ref[...])
        y_copy(i).start()

        @pl.when(i == pl.num_programs(0) - 1)
        def _():
            y_copy(i).wait()


def _conv_ffn(xn2, w_up, conv_w, conv_b, w_down, hres, gain, *, seq_len):
    m, d = hres.shape
    d_ff = w_down.shape[0]
    tm, tf = FFN_TM, FFN_TF
    nf = d_ff // tf
    assert nf >= 3
    tok = lambda i, f: (i, 0)
    return pl.pallas_call(
        functools.partial(_ffn_kernel, tm=tm, nf=nf, tiles_per_seq=seq_len // tm),
        grid=(m // tm, nf),
        in_specs=[
            pl.BlockSpec((tm, d), tok),
            pl.BlockSpec((d, tf), lambda i, f: (0, f)),
            pl.BlockSpec((d, tf), lambda i, f: (0, nf + f)),
            pl.BlockSpec((FFN_CONV, tf), lambda i, f: (0, f)),
            pl.BlockSpec((FFN_CONV, tf), lambda i, f: (0, nf + f)),
            pl.BlockSpec((1, tf), lambda i, f: (0, f)),
            pl.BlockSpec((1, tf), lambda i, f: (0, nf + f)),
            pl.BlockSpec((tf, d), lambda i, f: (f, 0)),
            pl.BlockSpec((1, d), lambda i, f: (0, 0)),
            pl.BlockSpec(memory_space=pl.ANY),
        ],
        out_specs=pl.BlockSpec(memory_space=pl.ANY),
        out_shape=jax.ShapeDtypeStruct((m, d), F32),
        scratch_shapes=[
            pltpu.VMEM((tm, d), F32),
            pltpu.VMEM((FFN_ROW_PARTS, tm // FFN_ROW_PARTS + SUBLANES, tf), F32),
            pltpu.VMEM((FFN_ROW_PARTS, tm // FFN_ROW_PARTS + SUBLANES, tf), F32),
            pltpu.VMEM((nf, SUBLANES, tf), F32),
            pltpu.VMEM((nf, SUBLANES, tf), F32),
            pltpu.VMEM((tm, d), F32),
            pltpu.SemaphoreType.DMA(()),
            pltpu.SemaphoreType.DMA(()),
        ],
        compiler_params=pltpu.CompilerParams(
            dimension_semantics=("arbitrary", "arbitrary"), vmem_limit_bytes=VMEM_LIMIT_BYTES),
        name="conv_ffn",
    )(xn2, w_up, w_up, conv_w, conv_w, conv_b, conv_b, w_down, gain, hres)


def _pad_lanes(vec):
    return jnp.pad(vec.astype(F32), (0, 128 - vec.shape[0]))[None, :]


def kernel(x, w_in, sb_out_gain, dn_conv_w, dn_a_log, dn_dt_bias, dn_out_gain, w_out, ln_mix_pre,
           ln_mix_post, w_up, ffn_conv_w, ffn_conv_b, w_down, ln_ffn_pre, ln_ffn_post):
    b, t, d = x.shape
    depth = w_in.shape[0]
    n_main = 3 * SB_WIDTH + 4 * DN_WIDTH
    assert t % DN_BLOCK == 0 and t % SB_TQ == 0 and t % FFN_TM == 0
    assert (b * t) % IN_TM == 0 and n_main % IN_TN == 0 and IN_TM % DN_BLOCK == 0
    hcols = lambda off: off // HEAD_DIM

    h2d = x.reshape(b * t, d)
    for l in range(depth):
        w_main = w_in[l, :, :n_main].astype(BF16)
        w_ba = jnp.pad(w_in[l, :, n_main:], ((0, 0), (0, 128 - 2 * DN_HEADS))).astype(BF16)
        alog = _pad_lanes(jnp.concatenate([jnp.zeros((DN_HEADS,), F32), dn_a_log[l]]))
        dtb = _pad_lanes(jnp.concatenate([jnp.zeros((DN_HEADS,), F32), dn_dt_bias[l]]))
        proj, gb, gbt = _in_proj(h2d, ln_mix_pre[l][None, :], w_main, w_ba, alog, dtb)
        proj3 = proj.reshape(b, t, n_main)

        o_sb = _sb_attention(proj3, sb_out_gain[l][None, :],
                             q_col=0, k_col=hcols(SB_WIDTH), v_col=hcols(2 * SB_WIDTH))
        dn0 = 3 * SB_WIDTH
        o_dn = _dn_scan(proj3, dn_conv_w[l], gb.reshape(b, t, 128), gbt, dn_out_gain[l][None, :],
                        q_col=hcols(dn0), k_col=hcols(dn0 + DN_WIDTH),
                        v_col=hcols(dn0 + 2 * DN_WIDTH), z_col=hcols(dn0 + 3 * DN_WIDTH))

        w_o = w_out[l].astype(BF16)
        hres, xn2 = _out_proj(o_sb.reshape(b * t, SB_WIDTH), o_dn.reshape(b * t, DN_WIDTH),
                              w_o[:SB_WIDTH], w_o[SB_WIDTH:], h2d,
                              ln_mix_post[l][None, :], ln_ffn_pre[l][None, :])

        h2d = _conv_ffn(xn2, w_up[l].astype(BF16), ffn_conv_w[l], ffn_conv_b[l][None, :],
                        w_down[l].astype(BF16), hres, ln_ffn_post[l][None, :], seq_len=t)
    return h2d.reshape(b, t, d)
```

```python
import functools

import jax
import jax.numpy as jnp
from jax import lax
from jax.experimental import pallas as pl
from jax.experimental.pallas import tpu as pltpu

F32 = jnp.float32
BF16 = jnp.bfloat16

HEAD_DIM = 128
SB_HEADS = 8
DN_HEADS = 8
SB_WIDTH = SB_HEADS * HEAD_DIM
DN_WIDTH = DN_HEADS * HEAD_DIM
SHORT_CONV = 4
FFN_CONV = 3
EPS = 1e-6

VMEM_LIMIT_BYTES = 56 * 1024 * 1024
IN_PROJ_VMEM_LIMIT_BYTES = 60 * 1024 * 1024
SUBLANES = 8
BF16_SUBLANES = 16

IN_TM, IN_TN = 1024, 1792
SB_TQ, SB_TK = 512, 256
SB_HEADS_PER_STEP = 8
LOG2E = 1.4426950408889634
DN_BLOCK = 256
DN_HEADS_PER_STEP = 8
OUT_TM = 512
OUT_ROW_PARTS = 2
FFN_TM, FFN_TF = 1024, 512
FFN_ROW_PARTS = 4


def _rms(x, gain):
    return x * lax.rsqrt(jnp.mean(x * x, axis=-1, keepdims=True) + EPS) * gain


def _dot(a, b):
    return jnp.dot(a, b, preferred_element_type=F32)


def _dot_nt(a, b):
    return lax.dot_general(a, b, (((1,), (1,)), ((), ())), preferred_element_type=F32)


def _dot_tn(a, b):
    return lax.dot_general(a, b, (((0,), (0,)), ((), ())), preferred_element_type=F32)


def _in_proj_kernel(x_ref, gain_ref, w_ref, cscale_ref, wba_ref, alog_ref, dtb_ref, *rest, tm, n_cast):
    cast_in = rest[:n_cast]
    proj_ref, gb_ref, gbt_ref = rest[n_cast:n_cast + 3]
    cast_out = rest[n_cast + 3:2 * n_cast + 3]
    xn_ref = rest[2 * n_cast + 3]
    j = pl.program_id(1)


    @pl.when(j == 0)
    def _():
        xn = _rms(x_ref[...], gain_ref[...]).astype(BF16)
        xn_ref[...] = xn
        ba = _dot(xn, wba_ref[...])
        lane = lax.broadcasted_iota(jnp.int32, ba.shape, 1)
        beta = jax.nn.sigmoid(ba)
        g = -(jnp.exp(alog_ref[...]) * jax.nn.softplus(ba + dtb_ref[...]))
        r = lax.broadcasted_iota(jnp.int32, (DN_BLOCK, DN_BLOCK), 0)
        c = lax.broadcasted_iota(jnp.int32, (DN_BLOCK, DN_BLOCK), 1)
        tri = (r >= c).astype(BF16)
        g1 = g.astype(BF16)
        rem = g - g1.astype(F32)
        g2 = rem.astype(BF16)
        g3 = (rem - g2.astype(F32)).astype(BF16)
        gsplit = jnp.concatenate([g1, g2, g3], axis=1)
        gc = []
        for blk0 in range(0, tm, DN_BLOCK):
            part = _dot(tri, gsplit[blk0:blk0 + DN_BLOCK, :])
            gc.append(part[:, 0:128] + part[:, 128:256] + part[:, 256:384])
        gb = jnp.where(lane < DN_HEADS, beta, jnp.concatenate(gc, axis=0))
        gb_ref[...] = gb
        gbt_ref[...] = gb.T

    proj_ref[...] = (_dot(xn_ref[...], w_ref[...]) * cscale_ref[...]).astype(proj_ref.dtype)

    for src, dst in zip(cast_in, cast_out):
        dst[...] = src[...].astype(dst.dtype)


def _in_proj(x2d, gain, w_all, n, w_ba, alog, dtb, later_weights):
    m, d = x2d.shape
    tm, tn = IN_TM, IN_TN
    gi, gj = m // tm, n // tn
    steps = gi * gj
    cscale = jnp.where(jnp.arange(n) < SB_WIDTH, HEAD_DIM ** -0.5 * LOG2E, 1.0).astype(F32)[None, :]
    slab = lambda w: pl.BlockSpec((w.shape[0] // steps, w.shape[1]), lambda i, j: (i * gj + j, 0))
    for w in later_weights:
        assert w.shape[0] % (steps * BF16_SUBLANES) == 0
    outs = pl.pallas_call(
        functools.partial(_in_proj_kernel, tm=tm, n_cast=len(later_weights)),
        grid=(gi, gj),
        in_specs=[
            pl.BlockSpec((tm, d), lambda i, j: (i, 0)),
            pl.BlockSpec((1, d), lambda i, j: (0, 0)),
            pl.BlockSpec((d, tn), lambda i, j: (0, j)),
            pl.BlockSpec((1, tn), lambda i, j: (0, j)),
            pl.BlockSpec((d, 128), lambda i, j: (0, 0)),
            pl.BlockSpec((1, 128), lambda i, j: (0, 0)),
            pl.BlockSpec((1, 128), lambda i, j: (0, 0)),
        ] + [slab(w) for w in later_weights],
        out_specs=[
            pl.BlockSpec((tm, tn), lambda i, j: (i, j)),
            pl.BlockSpec((tm, 128), lambda i, j: (i, 0)),
            pl.BlockSpec((128, tm), lambda i, j: (0, i)),
        ] + [slab(w) for w in later_weights],
        out_shape=[
            jax.ShapeDtypeStruct((m, n), BF16),
            jax.ShapeDtypeStruct((m, 128), F32),
            jax.ShapeDtypeStruct((128, m), F32),
        ] + [jax.ShapeDtypeStruct(w.shape, BF16) for w in later_weights],
        scratch_shapes=[pltpu.VMEM((tm, d), BF16)],
        compiler_params=pltpu.CompilerParams(
            dimension_semantics=("arbitrary", "arbitrary"),
            vmem_limit_bytes=IN_PROJ_VMEM_LIMIT_BYTES),
        name="in_proj",
    )(x2d, gain, w_all, cscale, w_ba, alog, dtb, *later_weights)
    return outs[0], outs[1], outs[2], outs[3:]


def _sb_kernel(q_ref, k_ref, v_ref, gain_ref, o_ref, acc_ref, run_ref, *, tq, tk, nh):
    qi = pl.program_id(2)
    r = lax.broadcasted_iota(jnp.int32, (tk, tk), 0)
    c = lax.broadcasted_iota(jnp.int32, (tk, tk), 1)
    suffix = (r >= c).astype(BF16)
    half = tk // 2
    heads = range(nh)
    cols = [slice(hh * HEAD_DIM, (hh + 1) * HEAD_DIM) for hh in heads]

    acc_ref[...] = jnp.zeros_like(acc_ref)
    run_ref[...] = jnp.zeros_like(run_ref)

    def tiles(kb, row0, nrows, causal):
        keys = pl.ds(kb * tk, tk)
        qrows = slice(row0, row0 + nrows)
        if causal:
            rr = lax.broadcasted_iota(jnp.int32, (nrows, tk), 0)
            cc = lax.broadcasted_iota(jnp.int32, (nrows, tk), 1)
            valid = cc < rr
        zs = [_dot_nt(q_ref[qrows, cols[hh]], k_ref[keys, cols[hh]]) for hh in heads]
        css = []
        for z in zs:
            sp = jnp.maximum(z, 0.0) + jnp.log(1.0 + jnp.exp2(-jnp.abs(z))) * LOG2E
            if causal:
                sp = jnp.where(valid, sp, 0.0)
            css.append(_dot(sp.astype(BF16), suffix))
        pvs, totals = [], []
        for hh in heads:
            z, cs, run = zs[hh], css[hh], run_ref[hh, qrows, :]
            args = [z[:, :half] - cs[:, :half] - run, z[:, half:] - cs[:, half:] - run]
            if causal:
                args = [jnp.where(valid[:, :half], args[0], -jnp.inf),
                        jnp.where(valid[:, half:], args[1], -jnp.inf)]
            att = jnp.concatenate([jnp.exp2(a.astype(BF16)) for a in args], axis=1)
            pvs.append(_dot(att, v_ref[keys, cols[hh]]))
            totals.append(jnp.broadcast_to(cs[:, 0:1], (nrows, HEAD_DIM)))
        for hh in heads:
            acc_ref[hh, qrows, :] += pvs[hh]
            run_ref[hh, qrows, :] += totals[hh]

    nk = tq // tk
    for j in reversed(range(nk)):
        tiles(qi * nk + j, j * tk, tq - j * tk, True)

    def body(step, carry):
        tiles(qi * nk - 1 - step, 0, tq, False)
        return carry

    lax.fori_loop(0, qi * nk, body, 0)
    gain = gain_ref[...]
    for hh in range(nh):
        o_ref[:, hh * HEAD_DIM:(hh + 1) * HEAD_DIM] = _rms(acc_ref[hh], gain).astype(o_ref.dtype)


def _sb_attention(proj3, gain, *, q_col, k_col, v_col):
    b, t, _ = proj3.shape
    tq, tk, nh = SB_TQ, SB_TK, SB_HEADS_PER_STEP
    assert tk == 2 * HEAD_DIM and tq % tk == 0
    width = nh * HEAD_DIM
    whole = lambda col: pl.BlockSpec((None, t, width), lambda bi, g, i: (bi, 0, col // nh + g),
                                     pipeline_mode=pl.Buffered(1))
    return pl.pallas_call(
        functools.partial(_sb_kernel, tq=tq, tk=tk, nh=nh),
        grid=(b, SB_HEADS // nh, t // tq),
        in_specs=[
            pl.BlockSpec((None, tq, width), lambda bi, g, i: (bi, i, q_col // nh + g)),
            whole(k_col),
            whole(v_col),
            pl.BlockSpec((1, HEAD_DIM), lambda bi, g, i: (0, 0)),
        ],
        out_specs=pl.BlockSpec((None, tq, width), lambda bi, g, i: (bi, i, g)),
        out_shape=jax.ShapeDtypeStruct((b, t, SB_WIDTH), BF16),
        scratch_shapes=[
            pltpu.VMEM((nh, tq, HEAD_DIM), F32),
            pltpu.VMEM((nh, tq, HEAD_DIM), F32),
        ],
        compiler_params=pltpu.CompilerParams(
            dimension_semantics=("arbitrary", "arbitrary", "arbitrary"),
            vmem_limit_bytes=VMEM_LIMIT_BYTES),
        name="sb_attn",
    )(proj3, proj3, proj3, gain)


def _dn_kernel(q_ref, k_ref, v_ref, z_ref, wq_ref, wk_ref, wv_ref, gb_ref, gbt_ref, gain_ref,
               o_ref, s_ref, bq_ref, bk_ref, bv_ref, *, blk, nh):
    grp = pl.program_id(1)
    i = pl.program_id(2)
    halo = SUBLANES
    heads = range(nh)
    cols = [slice(hh * HEAD_DIM, (hh + 1) * HEAD_DIM) for hh in heads]

    @pl.when(i == 0)
    def _():
        s_ref[...] = jnp.zeros_like(s_ref)
        for buf in (bq_ref, bk_ref, bv_ref):
            buf[0:halo, :] = jnp.zeros((halo, nh * HEAD_DIM), F32)

    def conv_silu(x_ref, w_ref, buf_ref):
        x = x_ref[...].astype(F32)
        buf_ref[halo:halo + blk, :] = x
        w = w_ref[...]
        y = x * w[3:4, :]
        for j in range(SHORT_CONV - 1):
            y = y + buf_ref[pl.ds(halo - (SHORT_CONV - 1) + j, blk), :] * w[j:j + 1, :]
        buf_ref[0:halo, :] = x[blk - halo:blk, :]
        return y * jax.nn.sigmoid(y)

    def l2n(x):
        return x * lax.rsqrt(jnp.sum(x * x, axis=-1, keepdims=True) + EPS)

    qc = conv_silu(q_ref, wq_ref, bq_ref)
    kc = conv_silu(k_ref, wk_ref, bk_ref)
    vc = conv_silu(v_ref, wv_ref, bv_ref)
    q = [l2n(qc[:, cols[hh]]) * (HEAD_DIM ** -0.5) for hh in heads]
    k = [l2n(kc[:, cols[hh]]) for hh in heads]
    v = [vc[:, cols[hh]] for hh in heads]

    gb = gb_ref[...]
    gbt = gbt_ref[...]
    lane = lax.broadcasted_iota(jnp.int32, gb.shape, 1)
    sub = lax.broadcasted_iota(jnp.int32, gbt.shape, 0)
    r = lax.broadcasted_iota(jnp.int32, (blk, blk), 0)
    c = lax.broadcasted_iota(jnp.int32, (blk, blk), 1)
    causal = r >= c

    beta, g_col, g_last, decay = [], [], [], []
    for hh in heads:
        head = grp * nh + hh
        beta.append(jnp.sum(jnp.where(lane == head, gb, 0.0), axis=1, keepdims=True))
        g_col.append(jnp.sum(jnp.where(lane == head + DN_HEADS, gb, 0.0), axis=1, keepdims=True))
        g_row = jnp.sum(jnp.where(sub == head + DN_HEADS, gbt, 0.0), axis=0, keepdims=True)
        g_last.append(g_row[:, blk - 1:blk])
        decay.append(jnp.where(causal, jnp.exp(jnp.where(causal, g_col[hh] - g_row, 0.0)), 0.0))

    k_b = [k[hh].astype(BF16) for hh in heads]
    kb = [k[hh] * beta[hh] for hh in heads]
    kk = [_dot_nt(kb[hh].astype(BF16), k_b[hh]) for hh in heads]
    qk = [_dot_nt(q[hh].astype(BF16), k_b[hh]) for hh in heads]
    a = [jnp.where(r > c, kk[hh] * decay[hh], 0.0) for hh in heads]
    qa = [(qk[hh] * decay[hh]).astype(BF16) for hh in heads]

    rc = r ^ c
    level = 31 - lax.clz(rc)
    eye = jnp.where(r == c, 1.0, 0.0)
    t = [eye - jnp.where(level == 0, a[hh], 0.0) for hh in heads]
    for p in range(1, blk.bit_length() - 1):
        t_b = [t[hh].astype(BF16) for hh in heads]
        cross = level == p
        prod = [_dot(jnp.where(cross, a[hh], 0.0).astype(BF16), t_b[hh]) for hh in heads]
        t = [t[hh] - _dot(t_b[hh], prod[hh].astype(BF16)) for hh in heads]

    eg = [jnp.exp(g_col[hh]) for hh in heads]
    t_b = [t[hh].astype(BF16) for hh in heads]
    u = [_dot(t_b[hh], (v[hh] * beta[hh]).astype(BF16)) for hh in heads]
    w = [_dot(t_b[hh], (kb[hh] * eg[hh]).astype(BF16)) for hh in heads]
    qd = [(q[hh] * eg[hh]).astype(BF16) for hh in heads]
    kt = [(k[hh] * jnp.exp(g_last[hh] - g_col[hh])).astype(BF16) for hh in heads]

    s = [s_ref[hh] for hh in heads]
    s_b = [s[hh].astype(BF16) for hh in heads]
    ws = [_dot(w[hh].astype(BF16), s_b[hh]) for hh in heads]
    qs = [_dot(qd[hh], s_b[hh]) for hh in heads]
    vn_b = [(u[hh] - ws[hh]).astype(BF16) for hh in heads]
    o = [qs[hh] + _dot(qa[hh], vn_b[hh]) for hh in heads]
    for hh in heads:
        s_ref[hh] = s[hh] * jnp.exp(g_last[hh]) + _dot_tn(kt[hh], vn_b[hh])

    gain = gain_ref[...]
    for hh in heads:
        zz = z_ref[:, cols[hh]].astype(F32)
        o_ref[:, cols[hh]] = (_rms(o[hh], gain) * (zz * jax.nn.sigmoid(zz))).astype(o_ref.dtype)


def _dn_scan(proj3, conv_w, gb3, gbt, gain, *, q_col, k_col, v_col, z_col):
    b, t, _ = proj3.shape
    blk, nh = DN_BLOCK, DN_HEADS_PER_STEP
    nblk = t // blk
    width = nh * HEAD_DIM
    tok = lambda col: pl.BlockSpec((None, blk, width), lambda bi, g, i: (bi, i, col // nh + g))
    cw = lambda col: pl.BlockSpec((SHORT_CONV, width), lambda bi, g, i: (0, col // nh + g))
    return pl.pallas_call(
        functools.partial(_dn_kernel, blk=blk, nh=nh),
        grid=(b, DN_HEADS // nh, nblk),
        in_specs=[
            tok(q_col), tok(k_col), tok(v_col), tok(z_col),
            cw(0), cw(DN_HEADS), cw(2 * DN_HEADS),
            pl.BlockSpec((None, blk, 128), lambda bi, g, i: (bi, i, 0)),
            pl.BlockSpec((128, blk), lambda bi, g, i: (0, bi * nblk + i)),
            pl.BlockSpec((1, HEAD_DIM), lambda bi, g, i: (0, 0)),
        ],
        out_specs=pl.BlockSpec((None, blk, width), lambda bi, g, i: (bi, i, g)),
        out_shape=jax.ShapeDtypeStruct((b, t, DN_WIDTH), BF16),
        scratch_shapes=[
            pltpu.VMEM((nh, HEAD_DIM, HEAD_DIM), F32),
            pltpu.VMEM((blk + SUBLANES, width), F32),
            pltpu.VMEM((blk + SUBLANES, width), F32),
            pltpu.VMEM((blk + SUBLANES, width), F32),
        ],
        compiler_params=pltpu.CompilerParams(
            dimension_semantics=("arbitrary", "arbitrary", "arbitrary"),
            vmem_limit_bytes=VMEM_LIMIT_BYTES),
        name="dn_scan",
    )(proj3, proj3, proj3, proj3, conv_w, conv_w, conv_w, gb3, gbt, gain)


def _out_proj_kernel(osb_ref, odn_ref, w1_ref, w2_ref, x_ref, gpost_ref, gpre_ref, h_ref, xn_ref):
    rows = osb_ref.shape[0] // OUT_ROW_PARTS
    for p in range(OUT_ROW_PARTS):
        rs = slice(p * rows, (p + 1) * rows)
        m = _dot(osb_ref[rs, :], w1_ref[...]) + _dot(odn_ref[rs, :], w2_ref[...])
        hres = x_ref[rs, :] + _rms(m, gpost_ref[...])
        h_ref[rs, :] = hres
        xn_ref[rs, :] = _rms(hres, gpre_ref[...]).astype(xn_ref.dtype)


def _out_proj(o_sb, o_dn, w_o, x2d, g_post, g_pre):
    m, d = x2d.shape
    tm = OUT_TM
    assert SB_WIDTH == DN_WIDTH
    row = lambda width: pl.BlockSpec((tm, width), lambda i: (i, 0))
    full = lambda a: pl.BlockSpec(a.shape, lambda i: (0, 0))
    w_half = lambda part: pl.BlockSpec((SB_WIDTH, d), lambda i: (part, 0))
    return pl.pallas_call(
        _out_proj_kernel,
        grid=(m // tm,),
        in_specs=[row(SB_WIDTH), row(DN_WIDTH), w_half(0), w_half(1), row(d), full(g_post), full(g_pre)],
        out_specs=[row(d), row(d)],
        out_shape=[jax.ShapeDtypeStruct((m, d), F32), jax.ShapeDtypeStruct((m, d), BF16)],
        compiler_params=pltpu.CompilerParams(
            dimension_semantics=("arbitrary",), vmem_limit_bytes=VMEM_LIMIT_BYTES),
        name="out_proj",
    )(o_sb, o_dn, w_o, w_o, x2d, g_post, g_pre)


def _ffn_kernel(xn_ref, wg_ref, wv_ref, cwg_ref, cwv_ref, bg_ref, bv_ref, wd_ref, gain_ref, h_hbm,
                y_hbm, acc_ref, bufg_ref, bufv_ref, tailg_ref, tailv_ref, hy_ref, h_sem, y_sem,
                *, tm, nf, tiles_per_seq):
    i = pl.program_id(0)
    f = pl.program_id(1)
    halo = SUBLANES
    parts = range(FFN_ROW_PARTS)
    rows = tm // FFN_ROW_PARTS
    seq_start = (i % tiles_per_seq) == 0

    def h_copy(tile):
        return pltpu.make_async_copy(h_hbm.at[pl.ds(tile * tm, tm), :], hy_ref, h_sem)

    def y_copy(tile):
        return pltpu.make_async_copy(hy_ref, y_hbm.at[pl.ds(tile * tm, tm), :], y_sem)

    @pl.when(f == 0)
    def _():
        acc_ref[...] = jnp.zeros_like(acc_ref)

    @pl.when(f == 1)
    def _():
        @pl.when(i > 0)
        def _():
            y_copy(i - 1).wait()
        h_copy(i).start()

    ups = []
    for p in parts:
        xn = xn_ref[p * rows:(p + 1) * rows, :]
        ups.append((_dot(xn, wg_ref[...]), _dot(xn, wv_ref[...])))

    def conv(p, which, cw_ref, b_ref, buf_ref, tail_ref):
        up = ups[p][which]
        prev = ups[p - 1][which][rows - halo:rows, :] if p else jnp.where(seq_start, 0.0, tail_ref[f])
        buf_ref[p, 0:halo, :] = prev
        buf_ref[p, halo:halo + rows, :] = up
        cw = cw_ref[...]
        out = up * cw[2:3, :] + b_ref[...]
        for j in range(FFN_CONV - 1):
            out = out + buf_ref[p, pl.ds(halo - (FFN_CONV - 1) + j, rows), :] * cw[j:j + 1, :]
        return out

    for p in parts:
        gate = conv(p, 0, cwg_ref, bg_ref, bufg_ref, tailg_ref)
        val = conv(p, 1, cwv_ref, bv_ref, bufv_ref, tailv_ref)
        act = (jax.nn.gelu(gate, approximate=True) * val).astype(BF16)
        acc_ref[p * rows:(p + 1) * rows, :] += _dot(act, wd_ref[...])
    tailg_ref[f] = ups[-1][0][rows - halo:rows, :]
    tailv_ref[f] = ups[-1][1][rows - halo:rows, :]

    @pl.when(f == nf - 1)
    def _():
        h_copy(i).wait()
        hy_ref[...] = hy_ref[...] + _rms(acc_ref[...], gain_ref[...])
        y_copy(i).start()

        @pl.when(i == pl.num_programs(0) - 1)
        def _():
            y_copy(i).wait()


def _conv_ffn(xn2, w_up, conv_w, conv_b, w_down, hres, gain, *, seq_len):
    m, d = hres.shape
    d_ff = w_down.shape[0]
    tm, tf = FFN_TM, FFN_TF
    nf = d_ff // tf
    assert nf >= 3
    tok = lambda i, f: (i, 0)
    return pl.pallas_call(
        functools.partial(_ffn_kernel, tm=tm, nf=nf, tiles_per_seq=seq_len // tm),
        grid=(m // tm, nf),
        in_specs=[
            pl.BlockSpec((tm, d), tok),
            pl.BlockSpec((d, tf), lambda i, f: (0, f)),
            pl.BlockSpec((d, tf), lambda i, f: (0, nf + f)),
            pl.BlockSpec((FFN_CONV, tf), lambda i, f: (0, f)),
            pl.BlockSpec((FFN_CONV, tf), lambda i, f: (0, nf + f)),
            pl.BlockSpec((1, tf), lambda i, f: (0, f)),
            pl.BlockSpec((1, tf), lambda i, f: (0, nf + f)),
            pl.BlockSpec((tf, d), lambda i, f: (f, 0)),
            pl.BlockSpec((1, d), lambda i, f: (0, 0)),
            pl.BlockSpec(memory_space=pl.ANY),
        ],
        out_specs=pl.BlockSpec(memory_space=pl.ANY),
        out_shape=jax.ShapeDtypeStruct((m, d), F32),
        scratch_shapes=[
            pltpu.VMEM((tm, d), F32),
            pltpu.VMEM((FFN_ROW_PARTS, tm // FFN_ROW_PARTS + SUBLANES, tf), F32),
            pltpu.VMEM((FFN_ROW_PARTS, tm // FFN_ROW_PARTS + SUBLANES, tf), F32),
            pltpu.VMEM((nf, SUBLANES, tf), F32),
            pltpu.VMEM((nf, SUBLANES, tf), F32),
            pltpu.VMEM((tm, d), F32),
            pltpu.SemaphoreType.DMA(()),
            pltpu.SemaphoreType.DMA(()),
        ],
        compiler_params=pltpu.CompilerParams(
            dimension_semantics=("arbitrary", "arbitrary"), vmem_limit_bytes=VMEM_LIMIT_BYTES),
        name="conv_ffn",
    )(xn2, w_up, w_up, conv_w, conv_w, conv_b, conv_b, w_down, gain, hres)


def _pad_lanes(vec):
    return jnp.pad(vec.astype(F32), (0, 128 - vec.shape[0]))[None, :]


def kernel(x, w_in, sb_out_gain, dn_conv_w, dn_a_log, dn_dt_bias, dn_out_gain, w_out, ln_mix_pre,
           ln_mix_post, w_up, ffn_conv_w, ffn_conv_b, w_down, ln_ffn_pre, ln_ffn_post):
    b, t, d = x.shape
    depth = w_in.shape[0]
    n_main = 3 * SB_WIDTH + 4 * DN_WIDTH
    assert t % DN_BLOCK == 0 and t % SB_TQ == 0 and t % FFN_TM == 0
    assert (b * t) % IN_TM == 0 and n_main % IN_TN == 0 and IN_TM % DN_BLOCK == 0
    hcols = lambda off: off // HEAD_DIM

    h2d = x.reshape(b * t, d)
    for l in range(depth):
        w_all = w_in[l].astype(BF16)
        w_ba = jnp.pad(w_in[l, :, n_main:], ((0, 0), (0, 128 - 2 * DN_HEADS))).astype(BF16)
        alog = _pad_lanes(jnp.concatenate([jnp.zeros((DN_HEADS,), F32), dn_a_log[l]]))
        dtb = _pad_lanes(jnp.concatenate([jnp.zeros((DN_HEADS,), F32), dn_dt_bias[l]]))
        d_ff = w_down.shape[1]
        later = (w_up[l], w_down[l].reshape(d, d_ff), w_out[l])
        proj, gb, gbt, (w_up_b, w_down_b, w_o) = _in_proj(
            h2d, ln_mix_pre[l][None, :], w_all, n_main, w_ba, alog, dtb, later)
        w_down_b = w_down_b.reshape(d_ff, d)
        proj3 = proj.reshape(b, t, n_main)

        o_sb = _sb_attention(proj3, sb_out_gain[l][None, :],
                             q_col=0, k_col=hcols(SB_WIDTH), v_col=hcols(2 * SB_WIDTH))
        dn0 = 3 * SB_WIDTH
        o_dn = _dn_scan(proj3, dn_conv_w[l], gb.reshape(b, t, 128), gbt, dn_out_gain[l][None, :],
                        q_col=hcols(dn0), k_col=hcols(dn0 + DN_WIDTH),
                        v_col=hcols(dn0 + 2 * DN_WIDTH), z_col=hcols(dn0 + 3 * DN_WIDTH))

        hres, xn2 = _out_proj(o_sb.reshape(b * t, SB_WIDTH), o_dn.reshape(b * t, DN_WIDTH),
                              w_o, h2d, ln_mix_post[l][None, :], ln_ffn_pre[l][None, :])

        h2d = _conv_ffn(xn2, w_up_b, ffn_conv_w[l], ffn_conv_b[l][None, :],
                        w_down_b, hres, ln_ffn_post[l][None, :], seq_len=t)
    return h2d.reshape(b, t, d)
```

```python
import functools

import jax
import jax.numpy as jnp
from jax import lax
from jax.experimental import pallas as pl
from jax.experimental.pallas import tpu as pltpu

F32 = jnp.float32
BF16 = jnp.bfloat16

HEAD_DIM = 128
SB_HEADS = 8
DN_HEADS = 8
SB_WIDTH = SB_HEADS * HEAD_DIM
DN_WIDTH = DN_HEADS * HEAD_DIM
SHORT_CONV = 4
FFN_CONV = 3
EPS = 1e-6

VMEM_LIMIT_BYTES = 56 * 1024 * 1024
IN_PROJ_VMEM_LIMIT_BYTES = 60 * 1024 * 1024
SUBLANES = 8
BF16_SUBLANES = 16

IN_TM, IN_TN = 1024, 1792
SB_TQ, SB_TK = 512, 256
SB_HEADS_PER_STEP = 8
LOG2E = 1.4426950408889634
DN_BLOCK = 256
DN_HEADS_PER_STEP = 8
OUT_TM = 512
OUT_ROW_PARTS = 2
FFN_TM, FFN_TF = 1024, 512
FFN_ROW_PARTS = 4


def _rms(x, gain):
    return x * lax.rsqrt(jnp.mean(x * x, axis=-1, keepdims=True) + EPS) * gain


def _dot(a, b):
    return jnp.dot(a, b, preferred_element_type=F32)


def _dot_nt(a, b):
    return lax.dot_general(a, b, (((1,), (1,)), ((), ())), preferred_element_type=F32)


def _dot_tn(a, b):
    return lax.dot_general(a, b, (((0,), (0,)), ((), ())), preferred_element_type=F32)


def _in_proj_kernel(x_ref, gain_ref, w_ref, cscale_ref, wba_ref, alog_ref, dtb_ref, *rest, tm, n_cast):
    cast_in = rest[:n_cast]
    proj_ref, gb_ref, gbt_ref = rest[n_cast:n_cast + 3]
    cast_out = rest[n_cast + 3:2 * n_cast + 3]
    xn_ref = rest[2 * n_cast + 3]
    j = pl.program_id(1)


    @pl.when(j == 0)
    def _():
        xn = _rms(x_ref[...], gain_ref[...]).astype(BF16)
        xn_ref[...] = xn
        ba = _dot(xn, wba_ref[...])
        lane = lax.broadcasted_iota(jnp.int32, ba.shape, 1)
        beta = jax.nn.sigmoid(ba)
        g = -(jnp.exp(alog_ref[...]) * jax.nn.softplus(ba + dtb_ref[...]))
        r = lax.broadcasted_iota(jnp.int32, (DN_BLOCK, DN_BLOCK), 0)
        c = lax.broadcasted_iota(jnp.int32, (DN_BLOCK, DN_BLOCK), 1)
        tri = (r >= c).astype(BF16)
        g1 = g.astype(BF16)
        rem = g - g1.astype(F32)
        g2 = rem.astype(BF16)
        g3 = (rem - g2.astype(F32)).astype(BF16)
        gsplit = jnp.concatenate([g1, g2, g3], axis=1)
        gc = []
        for blk0 in range(0, tm, DN_BLOCK):
            part = _dot(tri, gsplit[blk0:blk0 + DN_BLOCK, :])
            gc.append(part[:, 0:128] + part[:, 128:256] + part[:, 256:384])
        gb = jnp.where(lane < DN_HEADS, beta, jnp.concatenate(gc, axis=0))
        gb_ref[...] = gb
        gbt_ref[...] = gb.T

    proj_ref[...] = (_dot(xn_ref[...], w_ref[...]) * cscale_ref[...]).astype(proj_ref.dtype)

    for src, dst in zip(cast_in, cast_out):
        dst[...] = src[...].astype(dst.dtype)


def _in_proj(x2d, gain, w_all, n, w_ba, alog, dtb, later_weights, layer):
    m, d = x2d.shape
    tm, tn = IN_TM, IN_TN
    gi, gj = m // tm, n // tn
    steps = gi * gj
    cscale = jnp.where(jnp.arange(n) < SB_WIDTH, HEAD_DIM ** -0.5, 1.0).astype(F32)[None, :]

    def slab_specs(w):
        rows_total, cols = w.shape[1:]
        nslabs = next(s for s in (steps // k for k in (1, 2, 4, 8))
                      if rows_total % s == 0 and (rows_total // s) % BF16_SUBLANES == 0)
        rep, rows = steps // nslabs, rows_total // nslabs
        return (pl.BlockSpec((None, rows, cols), lambda i, j: (layer, (i * gj + j) // rep, 0)),
                pl.BlockSpec((rows, cols), lambda i, j: ((i * gj + j) // rep, 0)))

    specs = [slab_specs(w) for w in later_weights]
    outs = pl.pallas_call(
        functools.partial(_in_proj_kernel, tm=tm, n_cast=len(later_weights)),
        grid=(gi, gj),
        in_specs=[
            pl.BlockSpec((tm, d), lambda i, j: (i, 0)),
            pl.BlockSpec((1, d), lambda i, j: (0, 0)),
            pl.BlockSpec((d, tn), lambda i, j: (0, j)),
            pl.BlockSpec((1, tn), lambda i, j: (0, j)),
            pl.BlockSpec((d, 128), lambda i, j: (0, 0)),
            pl.BlockSpec((1, 128), lambda i, j: (0, 0)),
            pl.BlockSpec((1, 128), lambda i, j: (0, 0)),
        ] + [s[0] for s in specs],
        out_specs=[
            pl.BlockSpec((tm, tn), lambda i, j: (i, j)),
            pl.BlockSpec((tm, 128), lambda i, j: (i, 0)),
            pl.BlockSpec((128, tm), lambda i, j: (0, i)),
        ] + [s[1] for s in specs],
        out_shape=[
            jax.ShapeDtypeStruct((m, n), BF16),
            jax.ShapeDtypeStruct((m, 128), F32),
            jax.ShapeDtypeStruct((128, m), F32),
        ] + [jax.ShapeDtypeStruct(w.shape[1:], BF16) for w in later_weights],
        scratch_shapes=[pltpu.VMEM((tm, d), BF16)],
        compiler_params=pltpu.CompilerParams(
            dimension_semantics=("arbitrary", "arbitrary"),
            vmem_limit_bytes=IN_PROJ_VMEM_LIMIT_BYTES),
        name="in_proj",
    )(x2d, gain, w_all, cscale, w_ba, alog, dtb, *later_weights)
    return outs[0], outs[1], outs[2], outs[3:]


def _sb_kernel(q_ref, k_ref, v_ref, gain_ref, o_ref, acc_ref, run_ref, *, tq, tk, nh):
    qi = pl.program_id(2)
    r = lax.broadcasted_iota(jnp.int32, (tk, tk), 0)
    c = lax.broadcasted_iota(jnp.int32, (tk, tk), 1)
    suffix = (r >= c).astype(BF16)
    half = tk // 2
    heads = range(nh)
    cols = [slice(hh * HEAD_DIM, (hh + 1) * HEAD_DIM) for hh in heads]

    acc_ref[...] = jnp.zeros_like(acc_ref)
    run_ref[...] = jnp.zeros_like(run_ref)

    def tiles(kb, row0, nrows, causal):
        keys = pl.ds(kb * tk, tk)
        qrows = slice(row0, row0 + nrows)
        if causal:
            rr = lax.broadcasted_iota(jnp.int32, (nrows, tk), 0)
            cc = lax.broadcasted_iota(jnp.int32, (nrows, tk), 1)
            valid = cc < rr
        zs = [_dot_nt(q_ref[qrows, cols[hh]], k_ref[keys, cols[hh]]) for hh in heads]
        css = []
        for z in zs:
            sp = jnp.maximum(z, 0.0) + jnp.log(1.0 + jnp.exp2(jnp.abs(z) * (-LOG2E)))
            if causal:
                sp = jnp.where(valid, sp, 0.0)
            css.append(_dot(sp.astype(BF16), suffix))
        pvs, totals = [], []
        for hh in heads:
            z, cs, run = zs[hh], css[hh], run_ref[hh, qrows, :]
            args = [z[:, :half] - cs[:, :half] - run, z[:, half:] - cs[:, half:] - run]
            if causal:
                args = [jnp.where(valid[:, :half], args[0], -jnp.inf),
                        jnp.where(valid[:, half:], args[1], -jnp.inf)]
            att = jnp.concatenate([jnp.exp(a.astype(BF16)) for a in args], axis=1)
            pvs.append(_dot(att, v_ref[keys, cols[hh]]))
            totals.append(jnp.broadcast_to(cs[:, 0:1], (nrows, HEAD_DIM)))
        for hh in heads:
            acc_ref[hh, qrows, :] += pvs[hh]
            run_ref[hh, qrows, :] += totals[hh]

    nk = tq // tk
    for j in reversed(range(nk)):
        tiles(qi * nk + j, j * tk, tq - j * tk, True)

    def body(step, carry):
        tiles(qi * nk - 1 - step, 0, tq, False)
        return carry

    lax.fori_loop(0, qi * nk, body, 0)
    gain = gain_ref[...]
    for hh in range(nh):
        o_ref[:, hh * HEAD_DIM:(hh + 1) * HEAD_DIM] = _rms(acc_ref[hh], gain).astype(o_ref.dtype)


def _sb_attention(proj3, gain, *, q_col, k_col, v_col):
    b, t, _ = proj3.shape
    tq, tk, nh = SB_TQ, SB_TK, SB_HEADS_PER_STEP
    assert tk == 2 * HEAD_DIM and tq % tk == 0
    width = nh * HEAD_DIM
    whole = lambda col: pl.BlockSpec((None, t, width), lambda bi, g, i: (bi, 0, col // nh + g),
                                     pipeline_mode=pl.Buffered(1))
    return pl.pallas_call(
        functools.partial(_sb_kernel, tq=tq, tk=tk, nh=nh),
        grid=(b, SB_HEADS // nh, t // tq),
        in_specs=[
            pl.BlockSpec((None, tq, width), lambda bi, g, i: (bi, i, q_col // nh + g)),
            whole(k_col),
            whole(v_col),
            pl.BlockSpec((1, HEAD_DIM), lambda bi, g, i: (0, 0)),
        ],
        out_specs=pl.BlockSpec((None, tq, width), lambda bi, g, i: (bi, i, g)),
        out_shape=jax.ShapeDtypeStruct((b, t, SB_WIDTH), BF16),
        scratch_shapes=[
            pltpu.VMEM((nh, tq, HEAD_DIM), F32),
            pltpu.VMEM((nh, tq, HEAD_DIM), F32),
        ],
        compiler_params=pltpu.CompilerParams(
            dimension_semantics=("arbitrary", "arbitrary", "arbitrary"),
            vmem_limit_bytes=VMEM_LIMIT_BYTES),
        name="sb_attn",
    )(proj3, proj3, proj3, gain)


def _dn_kernel(q_ref, k_ref, v_ref, z_ref, wq_ref, wk_ref, wv_ref, gb_ref, gbt_ref, gain_ref,
               o_ref, s_ref, bq_ref, bk_ref, bv_ref, *, blk, nh):
    grp = pl.program_id(1)
    i = pl.program_id(2)
    halo = SUBLANES
    heads = range(nh)
    cols = [slice(hh * HEAD_DIM, (hh + 1) * HEAD_DIM) for hh in heads]

    @pl.when(i == 0)
    def _():
        s_ref[...] = jnp.zeros_like(s_ref)
        for buf in (bq_ref, bk_ref, bv_ref):
            buf[0:halo, :] = jnp.zeros((halo, nh * HEAD_DIM), F32)

    def conv_silu(x_ref, w_ref, buf_ref):
        x = x_ref[...].astype(F32)
        buf_ref[halo:halo + blk, :] = x
        w = w_ref[...]
        y = x * w[3:4, :]
        for j in range(SHORT_CONV - 1):
            y = y + buf_ref[pl.ds(halo - (SHORT_CONV - 1) + j, blk), :] * w[j:j + 1, :]
        buf_ref[0:halo, :] = x[blk - halo:blk, :]
        return y * jax.nn.sigmoid(y)

    def l2n(x):
        return x * lax.rsqrt(jnp.sum(x * x, axis=-1, keepdims=True) + EPS)

    qc = conv_silu(q_ref, wq_ref, bq_ref)
    kc = conv_silu(k_ref, wk_ref, bk_ref)
    vc = conv_silu(v_ref, wv_ref, bv_ref)
    q = [l2n(qc[:, cols[hh]]) * (HEAD_DIM ** -0.5) for hh in heads]
    k = [l2n(kc[:, cols[hh]]) for hh in heads]
    v = [vc[:, cols[hh]] for hh in heads]

    gb = gb_ref[...]
    gbt = gbt_ref[...]
    lane = lax.broadcasted_iota(jnp.int32, gb.shape, 1)
    sub = lax.broadcasted_iota(jnp.int32, gbt.shape, 0)
    r = lax.broadcasted_iota(jnp.int32, (blk, blk), 0)
    c = lax.broadcasted_iota(jnp.int32, (blk, blk), 1)
    causal = r >= c

    beta, g_col, g_last, decay = [], [], [], []
    for hh in heads:
        head = grp * nh + hh
        beta.append(jnp.sum(jnp.where(lane == head, gb, 0.0), axis=1, keepdims=True))
        g_col.append(jnp.sum(jnp.where(lane == head + DN_HEADS, gb, 0.0), axis=1, keepdims=True))
        g_row = jnp.sum(jnp.where(sub == head + DN_HEADS, gbt, 0.0), axis=0, keepdims=True)
        g_last.append(g_row[:, blk - 1:blk])
        decay.append(jnp.where(causal, jnp.exp(jnp.where(causal, g_col[hh] - g_row, 0.0)), 0.0))

    k_b = [k[hh].astype(BF16) for hh in heads]
    kb = [k[hh] * beta[hh] for hh in heads]
    kk = [_dot_nt(kb[hh].astype(BF16), k_b[hh]) for hh in heads]
    qk = [_dot_nt(q[hh].astype(BF16), k_b[hh]) for hh in heads]
    a = [jnp.where(r > c, kk[hh] * decay[hh], 0.0) for hh in heads]
    qa = [(qk[hh] * decay[hh]).astype(BF16) for hh in heads]

    rc = r ^ c
    level = 31 - lax.clz(rc)
    eye = jnp.where(r == c, 1.0, 0.0)
    t = [eye - jnp.where(level == 0, a[hh], 0.0) for hh in heads]
    for p in range(1, blk.bit_length() - 1):
        t_b = [t[hh].astype(BF16) for hh in heads]
        cross = level == p
        prod = [_dot(jnp.where(cross, a[hh], 0.0).astype(BF16), t_b[hh]) for hh in heads]
        t = [t[hh] - _dot(t_b[hh], prod[hh].astype(BF16)) for hh in heads]

    eg = [jnp.exp(g_col[hh]) for hh in heads]
    t_b = [t[hh].astype(BF16) for hh in heads]
    u = [_dot(t_b[hh], (v[hh] * beta[hh]).astype(BF16)) for hh in heads]
    w = [_dot(t_b[hh], (kb[hh] * eg[hh]).astype(BF16)) for hh in heads]
    qd = [(q[hh] * eg[hh]).astype(BF16) for hh in heads]
    kt = [(k[hh] * jnp.exp(g_last[hh] - g_col[hh])).astype(BF16) for hh in heads]

    s = [s_ref[hh] for hh in heads]
    s_b = [s[hh].astype(BF16) for hh in heads]
    ws = [_dot(w[hh].astype(BF16), s_b[hh]) for hh in heads]
    qs = [_dot(qd[hh], s_b[hh]) for hh in heads]
    vn_b = [(u[hh] - ws[hh]).astype(BF16) for hh in heads]
    o = [qs[hh] + _dot(qa[hh], vn_b[hh]) for hh in heads]
    for hh in heads:
        s_ref[hh] = s[hh] * jnp.exp(g_last[hh]) + _dot_tn(kt[hh], vn_b[hh])

    gain = gain_ref[...]
    for hh in heads:
        zz = z_ref[:, cols[hh]].astype(F32)
        o_ref[:, cols[hh]] = (_rms(o[hh], gain) * (zz * jax.nn.sigmoid(zz))).astype(o_ref.dtype)


def _dn_scan(proj3, conv_w, gb3, gbt, gain, *, q_col, k_col, v_col, z_col):
    b, t, _ = proj3.shape
    blk, nh = DN_BLOCK, DN_HEADS_PER_STEP
    nblk = t // blk
    width = nh * HEAD_DIM
    tok = lambda col: pl.BlockSpec((None, blk, width), lambda bi, g, i: (bi, i, col // nh + g))
    cw = lambda col: pl.BlockSpec((SHORT_CONV, width), lambda bi, g, i: (0, col // nh + g))
    return pl.pallas_call(
        functools.partial(_dn_kernel, blk=blk, nh=nh),
        grid=(b, DN_HEADS // nh, nblk),
        in_specs=[
            tok(q_col), tok(k_col), tok(v_col), tok(z_col),
            cw(0), cw(DN_HEADS), cw(2 * DN_HEADS),
            pl.BlockSpec((None, blk, 128), lambda bi, g, i: (bi, i, 0)),
            pl.BlockSpec((128, blk), lambda bi, g, i: (0, bi * nblk + i)),
            pl.BlockSpec((1, HEAD_DIM), lambda bi, g, i: (0, 0)),
        ],
        out_specs=pl.BlockSpec((None, blk, width), lambda bi, g, i: (bi, i, g)),
        out_shape=jax.ShapeDtypeStruct((b, t, DN_WIDTH), BF16),
        scratch_shapes=[
            pltpu.VMEM((nh, HEAD_DIM, HEAD_DIM), F32),
            pltpu.VMEM((blk + SUBLANES, width), F32),
            pltpu.VMEM((blk + SUBLANES, width), F32),
            pltpu.VMEM((blk + SUBLANES, width), F32),
        ],
        compiler_params=pltpu.CompilerParams(
            dimension_semantics=("arbitrary", "arbitrary", "arbitrary"),
            vmem_limit_bytes=VMEM_LIMIT_BYTES),
        name="dn_scan",
    )(proj3, proj3, proj3, proj3, conv_w, conv_w, conv_w, gb3, gbt, gain)


def _out_proj_kernel(osb_ref, odn_ref, w1_ref, w2_ref, x_ref, gpost_ref, gpre_ref, h_ref, xn_ref):
    rows = osb_ref.shape[0] // OUT_ROW_PARTS
    for p in range(OUT_ROW_PARTS):
        rs = slice(p * rows, (p + 1) * rows)
        m = _dot(osb_ref[rs, :], w1_ref[...]) + _dot(odn_ref[rs, :], w2_ref[...])
        hres = x_ref[rs, :] + _rms(m, gpost_ref[...])
        h_ref[rs, :] = hres
        xn_ref[rs, :] = _rms(hres, gpre_ref[...]).astype(xn_ref.dtype)


def _out_proj(o_sb, o_dn, w_o, x2d, g_post, g_pre):
    m, d = x2d.shape
    tm = OUT_TM
    assert SB_WIDTH == DN_WIDTH
    row = lambda width: pl.BlockSpec((tm, width), lambda i: (i, 0))
    full = lambda a: pl.BlockSpec(a.shape, lambda i: (0, 0))
    w_half = lambda part: pl.BlockSpec((SB_WIDTH, d), lambda i: (part, 0))
    return pl.pallas_call(
        _out_proj_kernel,
        grid=(m // tm,),
        in_specs=[row(SB_WIDTH), row(DN_WIDTH), w_half(0), w_half(1), row(d), full(g_post), full(g_pre)],
        out_specs=[row(d), row(d)],
        out_shape=[jax.ShapeDtypeStruct((m, d), F32), jax.ShapeDtypeStruct((m, d), BF16)],
        compiler_params=pltpu.CompilerParams(
            dimension_semantics=("arbitrary",), vmem_limit_bytes=VMEM_LIMIT_BYTES),
        name="out_proj",
    )(o_sb, o_dn, w_o, w_o, x2d, g_post, g_pre)


def _ffn_kernel(xn_ref, wg_ref, wv_ref, cwg_ref, cwv_ref, bg_ref, bv_ref, wd_ref, gain_ref, h_hbm,
                y_hbm, acc_ref, bufg_ref, bufv_ref, tailg_ref, tailv_ref, hy_ref, h_sem, y_sem,
                *, tm, nf, tiles_per_seq):
    i = pl.program_id(0)
    f = pl.program_id(1)
    halo = SUBLANES
    parts = range(FFN_ROW_PARTS)
    rows = tm // FFN_ROW_PARTS
    seq_start = (i % tiles_per_seq) == 0

    def h_copy(tile):
        return pltpu.make_async_copy(h_hbm.at[pl.ds(tile * tm, tm), :], hy_ref, h_sem)

    def y_copy(tile):
        return pltpu.make_async_copy(hy_ref, y_hbm.at[pl.ds(tile * tm, tm), :], y_sem)

    @pl.when(f == 0)
    def _():
        acc_ref[...] = jnp.zeros_like(acc_ref)

    @pl.when(f == 1)
    def _():
        @pl.when(i > 0)
        def _():
            y_copy(i - 1).wait()
        h_copy(i).start()

    ups = []
    for p in parts:
        xn = xn_ref[p * rows:(p + 1) * rows, :]
        ups.append((_dot(xn, wg_ref[...]), _dot(xn, wv_ref[...])))

    def conv(p, which, cw_ref, b_ref, buf_ref, tail_ref):
        up = ups[p][which]
        prev = ups[p - 1][which][rows - halo:rows, :] if p else jnp.where(seq_start, 0.0, tail_ref[f])
        buf_ref[p, 0:halo, :] = prev
        buf_ref[p, halo:halo + rows, :] = up
        cw = cw_ref[...]
        out = up * cw[2:3, :] + b_ref[...]
        for j in range(FFN_CONV - 1):
            out = out + buf_ref[p, pl.ds(halo - (FFN_CONV - 1) + j, rows), :] * cw[j:j + 1, :]
        return out

    for p in parts:
        gate = conv(p, 0, cwg_ref, bg_ref, bufg_ref, tailg_ref)
        val = conv(p, 1, cwv_ref, bv_ref, bufv_ref, tailv_ref)
        act = (jax.nn.gelu(gate, approximate=True) * val).astype(BF16)
        acc_ref[p * rows:(p + 1) * rows, :] += _dot(act, wd_ref[...])
    tailg_ref[f] = ups[-1][0][rows - halo:rows, :]
    tailv_ref[f] = ups[-1][1][rows - halo:rows, :]

    @pl.when(f == nf - 1)
    def _():
        h_copy(i).wait()
        hy_ref[...] = hy_ref[...] + _rms(acc_ref[...], gain_ref[...])
        y_copy(i).start()

        @pl.when(i == pl.num_programs(0) - 1)
        def _():
            y_copy(i).wait()


def _conv_ffn(xn2, w_up, conv_w, conv_b, w_down, hres, gain, *, seq_len):
    m, d = hres.shape
    d_ff = w_down.shape[0]
    tm, tf = FFN_TM, FFN_TF
    nf = d_ff // tf
    assert nf >= 3
    tok = lambda i, f: (i, 0)
    return pl.pallas_call(
        functools.partial(_ffn_kernel, tm=tm, nf=nf, tiles_per_seq=seq_len // tm),
        grid=(m // tm, nf),
        in_specs=[
            pl.BlockSpec((tm, d), tok),
            pl.BlockSpec((d, tf), lambda i, f: (0, f)),
            pl.BlockSpec((d, tf), lambda i, f: (0, nf + f)),
            pl.BlockSpec((FFN_CONV, tf), lambda i, f: (0, f)),
            pl.BlockSpec((FFN_CONV, tf), lambda i, f: (0, nf + f)),
            pl.BlockSpec((1, tf), lambda i, f: (0, f)),
            pl.BlockSpec((1, tf), lambda i, f: (0, nf + f)),
            pl.BlockSpec((tf, d), lambda i, f: (f, 0)),
            pl.BlockSpec((1, d), lambda i, f: (0, 0)),
            pl.BlockSpec(memory_space=pl.ANY),
        ],
        out_specs=pl.BlockSpec(memory_space=pl.ANY),
        out_shape=jax.ShapeDtypeStruct((m, d), F32),
        scratch_shapes=[
            pltpu.VMEM((tm, d), F32),
            pltpu.VMEM((FFN_ROW_PARTS, tm // FFN_ROW_PARTS + SUBLANES, tf), F32),
            pltpu.VMEM((FFN_ROW_PARTS, tm // FFN_ROW_PARTS + SUBLANES, tf), F32),
            pltpu.VMEM((nf, SUBLANES, tf), F32),
            pltpu.VMEM((nf, SUBLANES, tf), F32),
            pltpu.VMEM((tm, d), F32),
            pltpu.SemaphoreType.DMA(()),
            pltpu.SemaphoreType.DMA(()),
        ],
        compiler_params=pltpu.CompilerParams(
            dimension_semantics=("arbitrary", "arbitrary"), vmem_limit_bytes=VMEM_LIMIT_BYTES),
        name="conv_ffn",
    )(xn2, w_up, w_up, conv_w, conv_w, conv_b, conv_b, w_down, gain, hres)


def _pad_lanes(vec):
    return jnp.pad(vec.astype(F32), (0, 128 - vec.shape[0]))[None, :]


def kernel(x, w_in, sb_out_gain, dn_conv_w, dn_a_log, dn_dt_bias, dn_out_gain, w_out, ln_mix_pre,
           ln_mix_post, w_up, ffn_conv_w, ffn_conv_b, w_down, ln_ffn_pre, ln_ffn_post):
    b, t, d = x.shape
    depth = w_in.shape[0]
    n_main = 3 * SB_WIDTH + 4 * DN_WIDTH
    assert t % DN_BLOCK == 0 and t % SB_TQ == 0 and t % FFN_TM == 0
    assert (b * t) % IN_TM == 0 and n_main % IN_TN == 0 and IN_TM % DN_BLOCK == 0
    hcols = lambda off: off // HEAD_DIM

    h2d = x.reshape(b * t, d)
    for l in range(depth):
        w_all = w_in[l].astype(BF16)
        w_ba = jnp.pad(w_in[l, :, n_main:], ((0, 0), (0, 128 - 2 * DN_HEADS))).astype(BF16)
        alog = _pad_lanes(jnp.concatenate([jnp.zeros((DN_HEADS,), F32), dn_a_log[l]]))
        dtb = _pad_lanes(jnp.concatenate([jnp.zeros((DN_HEADS,), F32), dn_dt_bias[l]]))
        proj, gb, gbt, (w_up_b, w_down_b, w_o) = _in_proj(
            h2d, ln_mix_pre[l][None, :], w_all, n_main, w_ba, alog, dtb, (w_up, w_down, w_out), l)
        proj3 = proj.reshape(b, t, n_main)

        o_sb = _sb_attention(proj3, sb_out_gain[l][None, :],
                             q_col=0, k_col=hcols(SB_WIDTH), v_col=hcols(2 * SB_WIDTH))
        dn0 = 3 * SB_WIDTH
        o_dn = _dn_scan(proj3, dn_conv_w[l], gb.reshape(b, t, 128), gbt, dn_out_gain[l][None, :],
                        q_col=hcols(dn0), k_col=hcols(dn0 + DN_WIDTH),
                        v_col=hcols(dn0 + 2 * DN_WIDTH), z_col=hcols(dn0 + 3 * DN_WIDTH))

        hres, xn2 = _out_proj(o_sb.reshape(b * t, SB_WIDTH), o_dn.reshape(b * t, DN_WIDTH),
                              w_o, h2d, ln_mix_post[l][None, :], ln_ffn_pre[l][None, :])

        h2d = _conv_ffn(xn2, w_up_b, ffn_conv_w[l], ffn_conv_b[l][None, :],
                        w_down_b, hres, ln_ffn_post[l][None, :], seq_len=t)
    return h2d.reshape(b, t, d)
```

```python
import functools

import jax
import jax.numpy as jnp
from jax import lax
from jax.experimental import pallas as pl
from jax.experimental.pallas import tpu as pltpu

F32 = jnp.float32
BF16 = jnp.bfloat16

HEAD_DIM = 128
SB_HEADS = 8
DN_HEADS = 8
SB_WIDTH = SB_HEADS * HEAD_DIM
DN_WIDTH = DN_HEADS * HEAD_DIM
SHORT_CONV = 4
FFN_CONV = 3
EPS = 1e-6

VMEM_LIMIT_BYTES = 56 * 1024 * 1024
IN_PROJ_VMEM_LIMIT_BYTES = 60 * 1024 * 1024
SUBLANES = 8
BF16_SUBLANES = 16

IN_TM, IN_TN = 1024, 1792
IN_ROW_PARTS = 2
SB_TQ, SB_TK = 512, 256
SB_HEADS_PER_STEP = 8
LOG2E = 1.4426950408889634
DN_BLOCK = 256
DN_HEADS_PER_STEP = 8
OUT_TM = 512
OUT_ROW_PARTS = 2
FFN_TM, FFN_TF = 1024, 512
FFN_ROW_PARTS = 4


def _rms(x, gain):
    return x * lax.rsqrt(jnp.mean(x * x, axis=-1, keepdims=True) + EPS) * gain


def _dot(a, b):
    return jnp.dot(a, b, preferred_element_type=F32)


def _dot_nt(a, b):
    return lax.dot_general(a, b, (((1,), (1,)), ((), ())), preferred_element_type=F32)


def _dot_tn(a, b):
    return lax.dot_general(a, b, (((0,), (0,)), ((), ())), preferred_element_type=F32)


def _in_proj_kernel(x_ref, gain_ref, w_ref, cscale_ref, wba_ref, alog_ref, dtb_ref, *rest, tm, n_cast):
    cast_in = rest[:n_cast]
    proj_ref, gb_ref, gbt_ref = rest[n_cast:n_cast + 3]
    cast_out = rest[n_cast + 3:2 * n_cast + 3]
    xn_ref = rest[2 * n_cast + 3]
    j = pl.program_id(1)

    def convert_slabs():
        for src, dst in zip(cast_in, cast_out):
            dst[...] = src[...].astype(dst.dtype)

    def project(xn, rows):
        proj_ref[rows, :] = (_dot(xn, w_ref[...]) * cscale_ref[...]).astype(proj_ref.dtype)

    def gates(ba, rows):
        lane = lax.broadcasted_iota(jnp.int32, ba.shape, 1)
        beta = jax.nn.sigmoid(ba)
        g = -(jnp.exp(alog_ref[...]) * jax.nn.softplus(ba + dtb_ref[...]))
        r = lax.broadcasted_iota(jnp.int32, (DN_BLOCK, DN_BLOCK), 0)
        c = lax.broadcasted_iota(jnp.int32, (DN_BLOCK, DN_BLOCK), 1)
        tri = (r >= c).astype(BF16)
        g1 = g.astype(BF16)
        rem = g - g1.astype(F32)
        g2 = rem.astype(BF16)
        g3 = (rem - g2.astype(F32)).astype(BF16)
        gsplit = jnp.concatenate([g1, g2, g3], axis=1)
        gc = []
        for blk0 in range(0, ba.shape[0], DN_BLOCK):
            part = _dot(tri, gsplit[blk0:blk0 + DN_BLOCK, :])
            gc.append(part[:, 0:128] + part[:, 128:256] + part[:, 256:384])
        gb = jnp.where(lane < DN_HEADS, beta, jnp.concatenate(gc, axis=0))
        gb_ref[rows, :] = gb
        gbt_ref[:, rows] = gb.T

    @pl.when(j == 0)
    def _():
        nrows = tm // IN_ROW_PARTS
        for p in range(IN_ROW_PARTS):
            rows = slice(p * nrows, (p + 1) * nrows)
            xn = _rms(x_ref[rows, :], gain_ref[...]).astype(BF16)
            xn_ref[rows, :] = xn
            ba = _dot(xn, wba_ref[...])
            project(xn, rows)
            gates(ba, rows)
        convert_slabs()

    @pl.when(j > 0)
    def _():
        project(xn_ref[...], slice(None))
        convert_slabs()


def _in_proj(x2d, gain, w_all, n, w_ba, alog, dtb, later_weights, layer):
    m, d = x2d.shape
    tm, tn = IN_TM, IN_TN
    gi, gj = m // tm, n // tn
    steps = gi * gj
    cscale = jnp.where(jnp.arange(n) < SB_WIDTH, HEAD_DIM ** -0.5, 1.0).astype(F32)[None, :]

    def slab_specs(w):
        rows_total, cols = w.shape[1:]
        nslabs = next(s for s in (steps // k for k in (1, 2, 4, 8))
                      if rows_total % s == 0 and (rows_total // s) % BF16_SUBLANES == 0)
        rep, rows = steps // nslabs, rows_total // nslabs
        return (pl.BlockSpec((None, rows, cols), lambda i, j: (layer, (i * gj + j) // rep, 0)),
                pl.BlockSpec((rows, cols), lambda i, j: ((i * gj + j) // rep, 0)))

    specs = [slab_specs(w) for w in later_weights]
    outs = pl.pallas_call(
        functools.partial(_in_proj_kernel, tm=tm, n_cast=len(later_weights)),
        grid=(gi, gj),
        in_specs=[
            pl.BlockSpec((tm, d), lambda i, j: (i, 0)),
            pl.BlockSpec((1, d), lambda i, j: (0, 0)),
            pl.BlockSpec((d, tn), lambda i, j: (0, j)),
            pl.BlockSpec((1, tn), lambda i, j: (0, j)),
            pl.BlockSpec((d, 128), lambda i, j: (0, 0)),
            pl.BlockSpec((1, 128), lambda i, j: (0, 0)),
            pl.BlockSpec((1, 128), lambda i, j: (0, 0)),
        ] + [s[0] for s in specs],
        out_specs=[
            pl.BlockSpec((tm, tn), lambda i, j: (i, j)),
            pl.BlockSpec((tm, 128), lambda i, j: (i, 0)),
            pl.BlockSpec((128, tm), lambda i, j: (0, i)),
        ] + [s[1] for s in specs],
        out_shape=[
            jax.ShapeDtypeStruct((m, n), BF16),
            jax.ShapeDtypeStruct((m, 128), F32),
            jax.ShapeDtypeStruct((128, m), F32),
        ] + [jax.ShapeDtypeStruct(w.shape[1:], BF16) for w in later_weights],
        scratch_shapes=[pltpu.VMEM((tm, d), BF16)],
        compiler_params=pltpu.CompilerParams(
            dimension_semantics=("arbitrary", "arbitrary"),
            vmem_limit_bytes=IN_PROJ_VMEM_LIMIT_BYTES),
        name="in_proj",
    )(x2d, gain, w_all, cscale, w_ba, alog, dtb, *later_weights)
    return outs[0], outs[1], outs[2], outs[3:]


def _sb_kernel(q_ref, k_ref, v_ref, gain_ref, o_ref, acc_ref, run_ref, *, tq, tk, nh):
    qi = pl.program_id(2)
    r = lax.broadcasted_iota(jnp.int32, (tk, tk), 0)
    c = lax.broadcasted_iota(jnp.int32, (tk, tk), 1)
    suffix = (r >= c).astype(BF16)
    half = tk // 2
    heads = range(nh)
    cols = [slice(hh * HEAD_DIM, (hh + 1) * HEAD_DIM) for hh in heads]

    acc_ref[...] = jnp.zeros_like(acc_ref)
    run_ref[...] = jnp.zeros_like(run_ref)

    def tiles(kb, row0, nrows, causal):
        keys = pl.ds(kb * tk, tk)
        qrows = slice(row0, row0 + nrows)
        if causal:
            rr = lax.broadcasted_iota(jnp.int32, (nrows, tk), 0)
            cc = lax.broadcasted_iota(jnp.int32, (nrows, tk), 1)
            valid = cc < rr
        zs = [_dot_nt(q_ref[qrows, cols[hh]], k_ref[keys, cols[hh]]) for hh in heads]
        css = []
        for z in zs:
            sp = jnp.maximum(z, 0.0) + jnp.log(1.0 + jnp.exp2(jnp.abs(z) * (-LOG2E)))
            if causal:
                sp = jnp.where(valid, sp, 0.0)
            css.append(_dot(sp.astype(BF16), suffix))
        pvs, totals = [], []
        for hh in heads:
            z, cs, run = zs[hh], css[hh], run_ref[hh, qrows, :]
            args = [z[:, :half] - cs[:, :half] - run, z[:, half:] - cs[:, half:] - run]
            if causal:
                args = [jnp.where(valid[:, :half], args[0], -jnp.inf),
                        jnp.where(valid[:, half:], args[1], -jnp.inf)]
            att = jnp.concatenate([jnp.exp(a.astype(BF16)) for a in args], axis=1)
            pvs.append(_dot(att, v_ref[keys, cols[hh]]))
            totals.append(jnp.broadcast_to(cs[:, 0:1], (nrows, HEAD_DIM)))
        for hh in heads:
            acc_ref[hh, qrows, :] += pvs[hh]
            run_ref[hh, qrows, :] += totals[hh]

    nk = tq // tk
    for j in reversed(range(nk)):
        tiles(qi * nk + j, j * tk, tq - j * tk, True)

    def body(step, carry):
        tiles(qi * nk - 1 - step, 0, tq, False)
        return carry

    lax.fori_loop(0, qi * nk, body, 0)
    gain = gain_ref[...]
    for hh in range(nh):
        o_ref[:, hh * HEAD_DIM:(hh + 1) * HEAD_DIM] = _rms(acc_ref[hh], gain).astype(o_ref.dtype)


def _sb_attention(proj3, gain, *, q_col, k_col, v_col):
    b, t, _ = proj3.shape
    tq, tk, nh = SB_TQ, SB_TK, SB_HEADS_PER_STEP
    assert tk == 2 * HEAD_DIM and tq % tk == 0
    width = nh * HEAD_DIM
    whole = lambda col: pl.BlockSpec((None, t, width), lambda bi, g, i: (bi, 0, col // nh + g),
                                     pipeline_mode=pl.Buffered(1))
    return pl.pallas_call(
        functools.partial(_sb_kernel, tq=tq, tk=tk, nh=nh),
        grid=(b, SB_HEADS // nh, t // tq),
        in_specs=[
            pl.BlockSpec((None, tq, width), lambda bi, g, i: (bi, i, q_col // nh + g)),
            whole(k_col),
            whole(v_col),
            pl.BlockSpec((1, HEAD_DIM), lambda bi, g, i: (0, 0)),
        ],
        out_specs=pl.BlockSpec((None, tq, width), lambda bi, g, i: (bi, i, g)),
        out_shape=jax.ShapeDtypeStruct((b, t, SB_WIDTH), BF16),
        scratch_shapes=[
            pltpu.VMEM((nh, tq, HEAD_DIM), F32),
            pltpu.VMEM((nh, tq, HEAD_DIM), F32),
        ],
        compiler_params=pltpu.CompilerParams(
            dimension_semantics=("arbitrary", "arbitrary", "arbitrary"),
            vmem_limit_bytes=VMEM_LIMIT_BYTES),
        name="sb_attn",
    )(proj3, proj3, proj3, gain)


def _dn_kernel(q_ref, k_ref, v_ref, z_ref, wq_ref, wk_ref, wv_ref, gb_ref, gbt_ref, gain_ref,
               o_ref, s_ref, bq_ref, bk_ref, bv_ref, *, blk, nh):
    grp = pl.program_id(1)
    i = pl.program_id(2)
    halo = SUBLANES
    heads = range(nh)
    cols = [slice(hh * HEAD_DIM, (hh + 1) * HEAD_DIM) for hh in heads]

    @pl.when(i == 0)
    def _():
        s_ref[...] = jnp.zeros_like(s_ref)
        for buf in (bq_ref, bk_ref, bv_ref):
            buf[0:halo, :] = jnp.zeros((halo, nh * HEAD_DIM), F32)

    def conv_silu(x_ref, w_ref, buf_ref):
        x = x_ref[...].astype(F32)
        buf_ref[halo:halo + blk, :] = x
        w = w_ref[...]
        y = x * w[3:4, :]
        for j in range(SHORT_CONV - 1):
            y = y + buf_ref[pl.ds(halo - (SHORT_CONV - 1) + j, blk), :] * w[j:j + 1, :]
        buf_ref[0:halo, :] = x[blk - halo:blk, :]
        return y * jax.nn.sigmoid(y)

    def l2n(x):
        return x * lax.rsqrt(jnp.sum(x * x, axis=-1, keepdims=True) + EPS)

    qc = conv_silu(q_ref, wq_ref, bq_ref)
    kc = conv_silu(k_ref, wk_ref, bk_ref)
    vc = conv_silu(v_ref, wv_ref, bv_ref)
    q = [l2n(qc[:, cols[hh]]) * (HEAD_DIM ** -0.5) for hh in heads]
    k = [l2n(kc[:, cols[hh]]) for hh in heads]
    v = [vc[:, cols[hh]] for hh in heads]

    gb = gb_ref[...]
    gbt = gbt_ref[...]
    lane = lax.broadcasted_iota(jnp.int32, gb.shape, 1)
    sub = lax.broadcasted_iota(jnp.int32, gbt.shape, 0)
    r = lax.broadcasted_iota(jnp.int32, (blk, blk), 0)
    c = lax.broadcasted_iota(jnp.int32, (blk, blk), 1)
    causal = r >= c

    beta, g_col, g_last, decay = [], [], [], []
    for hh in heads:
        head = grp * nh + hh
        beta.append(jnp.sum(jnp.where(lane == head, gb, 0.0), axis=1, keepdims=True))
        g_col.append(jnp.sum(jnp.where(lane == head + DN_HEADS, gb, 0.0), axis=1, keepdims=True))
        g_row = jnp.sum(jnp.where(sub == head + DN_HEADS, gbt, 0.0), axis=0, keepdims=True)
        g_last.append(g_row[:, blk - 1:blk])
        decay.append(jnp.where(causal, jnp.exp(jnp.where(causal, g_col[hh] - g_row, 0.0)), 0.0))

    k_b = [k[hh].astype(BF16) for hh in heads]
    kb = [k[hh] * beta[hh] for hh in heads]
    kk = [_dot_nt(kb[hh].astype(BF16), k_b[hh]) for hh in heads]
    qk = [_dot_nt(q[hh].astype(BF16), k_b[hh]) for hh in heads]
    a = [jnp.where(r > c, kk[hh] * decay[hh], 0.0) for hh in heads]
    qa = [(qk[hh] * decay[hh]).astype(BF16) for hh in heads]

    rc = r ^ c
    level = 31 - lax.clz(rc)
    eye = jnp.where(r == c, 1.0, 0.0)
    t = [eye - jnp.where(level == 0, a[hh], 0.0) for hh in heads]
    for p in range(1, blk.bit_length() - 1):
        t_b = [t[hh].astype(BF16) for hh in heads]
        cross = level == p
        prod = [_dot(jnp.where(cross, a[hh], 0.0).astype(BF16), t_b[hh]) for hh in heads]
        t = [t[hh] - _dot(t_b[hh], prod[hh].astype(BF16)) for hh in heads]

    eg = [jnp.exp(g_col[hh]) for hh in heads]
    t_b = [t[hh].astype(BF16) for hh in heads]
    u = [_dot(t_b[hh], (v[hh] * beta[hh]).astype(BF16)) for hh in heads]
    w = [_dot(t_b[hh], (kb[hh] * eg[hh]).astype(BF16)) for hh in heads]
    qd = [(q[hh] * eg[hh]).astype(BF16) for hh in heads]
    kt = [(k[hh] * jnp.exp(g_last[hh] - g_col[hh])).astype(BF16) for hh in heads]

    s = [s_ref[hh] for hh in heads]
    s_b = [s[hh].astype(BF16) for hh in heads]
    ws = [_dot(w[hh].astype(BF16), s_b[hh]) for hh in heads]
    qs = [_dot(qd[hh], s_b[hh]) for hh in heads]
    vn_b = [(u[hh] - ws[hh]).astype(BF16) for hh in heads]
    o = [qs[hh] + _dot(qa[hh], vn_b[hh]) for hh in heads]
    for hh in heads:
        s_ref[hh] = s[hh] * jnp.exp(g_last[hh]) + _dot_tn(kt[hh], vn_b[hh])

    gain = gain_ref[...]
    for hh in heads:
        zz = z_ref[:, cols[hh]].astype(F32)
        o_ref[:, cols[hh]] = (_rms(o[hh], gain) * (zz * jax.nn.sigmoid(zz))).astype(o_ref.dtype)


def _dn_scan(proj3, conv_w, gb3, gbt, gain, *, q_col, k_col, v_col, z_col):
    b, t, _ = proj3.shape
    blk, nh = DN_BLOCK, DN_HEADS_PER_STEP
    nblk = t // blk
    width = nh * HEAD_DIM
    tok = lambda col: pl.BlockSpec((None, blk, width), lambda bi, g, i: (bi, i, col // nh + g))
    cw = lambda col: pl.BlockSpec((SHORT_CONV, width), lambda bi, g, i: (0, col // nh + g))
    return pl.pallas_call(
        functools.partial(_dn_kernel, blk=blk, nh=nh),
        grid=(b, DN_HEADS // nh, nblk),
        in_specs=[
            tok(q_col), tok(k_col), tok(v_col), tok(z_col),
            cw(0), cw(DN_HEADS), cw(2 * DN_HEADS),
            pl.BlockSpec((None, blk, 128), lambda bi, g, i: (bi, i, 0)),
            pl.BlockSpec((128, blk), lambda bi, g, i: (0, bi * nblk + i)),
            pl.BlockSpec((1, HEAD_DIM), lambda bi, g, i: (0, 0)),
        ],
        out_specs=pl.BlockSpec((None, blk, width), lambda bi, g, i: (bi, i, g)),
        out_shape=jax.ShapeDtypeStruct((b, t, DN_WIDTH), BF16),
        scratch_shapes=[
            pltpu.VMEM((nh, HEAD_DIM, HEAD_DIM), F32),
            pltpu.VMEM((blk + SUBLANES, width), F32),
            pltpu.VMEM((blk + SUBLANES, width), F32),
            pltpu.VMEM((blk + SUBLANES, width), F32),
        ],
        compiler_params=pltpu.CompilerParams(
            dimension_semantics=("arbitrary", "arbitrary", "arbitrary"),
            vmem_limit_bytes=VMEM_LIMIT_BYTES),
        name="dn_scan",
    )(proj3, proj3, proj3, proj3, conv_w, conv_w, conv_w, gb3, gbt, gain)


def _out_proj_kernel(osb_ref, odn_ref, w1_ref, w2_ref, x_ref, gpost_ref, gpre_ref, h_ref, xn_ref):
    rows = osb_ref.shape[0] // OUT_ROW_PARTS
    for p in range(OUT_ROW_PARTS):
        rs = slice(p * rows, (p + 1) * rows)
        m = _dot(osb_ref[rs, :], w1_ref[...]) + _dot(odn_ref[rs, :], w2_ref[...])
        hres = x_ref[rs, :] + _rms(m, gpost_ref[...])
        h_ref[rs, :] = hres
        xn_ref[rs, :] = _rms(hres, gpre_ref[...]).astype(xn_ref.dtype)


def _out_proj(o_sb, o_dn, w_o, x2d, g_post, g_pre):
    m, d = x2d.shape
    tm = OUT_TM
    assert SB_WIDTH == DN_WIDTH
    row = lambda width: pl.BlockSpec((tm, width), lambda i: (i, 0))
    full = lambda a: pl.BlockSpec(a.shape, lambda i: (0, 0))
    w_half = lambda part: pl.BlockSpec((SB_WIDTH, d), lambda i: (part, 0))
    return pl.pallas_call(
        _out_proj_kernel,
        grid=(m // tm,),
        in_specs=[row(SB_WIDTH), row(DN_WIDTH), w_half(0), w_half(1), row(d), full(g_post), full(g_pre)],
        out_specs=[row(d), row(d)],
        out_shape=[jax.ShapeDtypeStruct((m, d), F32), jax.ShapeDtypeStruct((m, d), BF16)],
        compiler_params=pltpu.CompilerParams(
            dimension_semantics=("arbitrary",), vmem_limit_bytes=VMEM_LIMIT_BYTES),
        name="out_proj",
    )(o_sb, o_dn, w_o, w_o, x2d, g_post, g_pre)


def _ffn_kernel(xn_ref, wg_ref, wv_ref, cwg_ref, cwv_ref, bg_ref, bv_ref, wd_ref, gain_ref, h_hbm,
                y_hbm, acc_ref, bufg_ref, bufv_ref, tailg_ref, tailv_ref, hy_ref, h_sem, y_sem,
                *, tm, nf, tiles_per_seq):
    i = pl.program_id(0)
    f = pl.program_id(1)
    halo = SUBLANES
    parts = range(FFN_ROW_PARTS)
    rows = tm // FFN_ROW_PARTS
    seq_start = (i % tiles_per_seq) == 0

    def h_copy(tile):
        return pltpu.make_async_copy(h_hbm.at[pl.ds(tile * tm, tm), :], hy_ref, h_sem)

    def y_copy(tile):
        return pltpu.make_async_copy(hy_ref, y_hbm.at[pl.ds(tile * tm, tm), :], y_sem)

    @pl.when(f == 0)
    def _():
        acc_ref[...] = jnp.zeros_like(acc_ref)

    @pl.when(f == 1)
    def _():
        @pl.when(i > 0)
        def _():
            y_copy(i - 1).wait()
        h_copy(i).start()

    ups = []
    for p in parts:
        xn = xn_ref[p * rows:(p + 1) * rows, :]
        ups.append((_dot(xn, wg_ref[...]), _dot(xn, wv_ref[...])))

    def conv(p, which, cw_ref, b_ref, buf_ref, tail_ref):
        up = ups[p][which]
        prev = ups[p - 1][which][rows - halo:rows, :] if p else jnp.where(seq_start, 0.0, tail_ref[f])
        buf_ref[p, 0:halo, :] = prev
        buf_ref[p, halo:halo + rows, :] = up
        cw = cw_ref[...]
        out = up * cw[2:3, :] + b_ref[...]
        for j in range(FFN_CONV - 1):
            out = out + buf_ref[p, pl.ds(halo - (FFN_CONV - 1) + j, rows), :] * cw[j:j + 1, :]
        return out

    for p in parts:
        gate = conv(p, 0, cwg_ref, bg_ref, bufg_ref, tailg_ref)
        val = conv(p, 1, cwv_ref, bv_ref, bufv_ref, tailv_ref)
        act = (jax.nn.gelu(gate, approximate=True) * val).astype(BF16)
        acc_ref[p * rows:(p + 1) * rows, :] += _dot(act, wd_ref[...])
    tailg_ref[f] = ups[-1][0][rows - halo:rows, :]
    tailv_ref[f] = ups[-1][1][rows - halo:rows, :]

    @pl.when(f == nf - 1)
    def _():
        h_copy(i).wait()
        hy_ref[...] = hy_ref[...] + _rms(acc_ref[...], gain_ref[...])
        y_copy(i).start()

        @pl.when(i == pl.num_programs(0) - 1)
        def _():
            y_copy(i).wait()


def _conv_ffn(xn2, w_up, conv_w, conv_b, w_down, hres, gain, *, seq_len):
    m, d = hres.shape
    d_ff = w_down.shape[0]
    tm, tf = FFN_TM, FFN_TF
    nf = d_ff // tf
    assert nf >= 3
    tok = lambda i, f: (i, 0)
    return pl.pallas_call(
        functools.partial(_ffn_kernel, tm=tm, nf=nf, tiles_per_seq=seq_len // tm),
        grid=(m // tm, nf),
        in_specs=[
            pl.BlockSpec((tm, d), tok),
            pl.BlockSpec((d, tf), lambda i, f: (0, f)),
            pl.BlockSpec((d, tf), lambda i, f: (0, nf + f)),
            pl.BlockSpec((FFN_CONV, tf), lambda i, f: (0, f)),
            pl.BlockSpec((FFN_CONV, tf), lambda i, f: (0, nf + f)),
            pl.BlockSpec((1, tf), lambda i, f: (0, f)),
            pl.BlockSpec((1, tf), lambda i, f: (0, nf + f)),
            pl.BlockSpec((tf, d), lambda i, f: (f, 0)),
            pl.BlockSpec((1, d), lambda i, f: (0, 0)),
            pl.BlockSpec(memory_space=pl.ANY),
        ],
        out_specs=pl.BlockSpec(memory_space=pl.ANY),
        out_shape=jax.ShapeDtypeStruct((m, d), F32),
        scratch_shapes=[
            pltpu.VMEM((tm, d), F32),
            pltpu.VMEM((FFN_ROW_PARTS, tm // FFN_ROW_PARTS + SUBLANES, tf), F32),
            pltpu.VMEM((FFN_ROW_PARTS, tm // FFN_ROW_PARTS + SUBLANES, tf), F32),
            pltpu.VMEM((nf, SUBLANES, tf), F32),
            pltpu.VMEM((nf, SUBLANES, tf), F32),
            pltpu.VMEM((tm, d), F32),
            pltpu.SemaphoreType.DMA(()),
            pltpu.SemaphoreType.DMA(()),
        ],
        compiler_params=pltpu.CompilerParams(
            dimension_semantics=("arbitrary", "arbitrary"), vmem_limit_bytes=VMEM_LIMIT_BYTES),
        name="conv_ffn",
    )(xn2, w_up, w_up, conv_w, conv_w, conv_b, conv_b, w_down, gain, hres)


def _pad_lanes(vec):
    return jnp.pad(vec.astype(F32), (0, 128 - vec.shape[0]))[None, :]


def kernel(x, w_in, sb_out_gain, dn_conv_w, dn_a_log, dn_dt_bias, dn_out_gain, w_out, ln_mix_pre,
           ln_mix_post, w_up, ffn_conv_w, ffn_conv_b, w_down, ln_ffn_pre, ln_ffn_post):
    b, t, d = x.shape
    depth = w_in.shape[0]
    n_main = 3 * SB_WIDTH + 4 * DN_WIDTH
    assert t % DN_BLOCK == 0 and t % SB_TQ == 0 and t % FFN_TM == 0
    assert (b * t) % IN_TM == 0 and n_main % IN_TN == 0 and IN_TM % DN_BLOCK == 0
    hcols = lambda off: off // HEAD_DIM

    h2d = x.reshape(b * t, d)
    for l in range(depth):
        w_all = w_in[l].astype(BF16)
        w_ba = jnp.pad(w_in[l, :, n_main:], ((0, 0), (0, 128 - 2 * DN_HEADS))).astype(BF16)
        alog = _pad_lanes(jnp.concatenate([jnp.zeros((DN_HEADS,), F32), dn_a_log[l]]))
        dtb = _pad_lanes(jnp.concatenate([jnp.zeros((DN_HEADS,), F32), dn_dt_bias[l]]))
        proj, gb, gbt, (w_up_b, w_down_b, w_o) = _in_proj(
            h2d, ln_mix_pre[l][None, :], w_all, n_main, w_ba, alog, dtb, (w_up, w_down, w_out), l)
        proj3 = proj.reshape(b, t, n_main)

        o_sb = _sb_attention(proj3, sb_out_gain[l][None, :],
                             q_col=0, k_col=hcols(SB_WIDTH), v_col=hcols(2 * SB_WIDTH))
        dn0 = 3 * SB_WIDTH
        o_dn = _dn_scan(proj3, dn_conv_w[l], gb.reshape(b, t, 128), gbt, dn_out_gain[l][None, :],
                        q_col=hcols(dn0), k_col=hcols(dn0 + DN_WIDTH),
                        v_col=hcols(dn0 + 2 * DN_WIDTH), z_col=hcols(dn0 + 3 * DN_WIDTH))

        hres, xn2 = _out_proj(o_sb.reshape(b * t, SB_WIDTH), o_dn.reshape(b * t, DN_WIDTH),
                              w_o, h2d, ln_mix_post[l][None, :], ln_ffn_pre[l][None, :])

        h2d = _conv_ffn(xn2, w_up_b, ffn_conv_w[l], ffn_conv_b[l][None, :],
                        w_down_b, hres, ln_ffn_post[l][None, :], seq_len=t)
    return h2d.reshape(b, t, d)
```

```python
import functools

import jax
import jax.numpy as jnp
from jax import lax
from jax.experimental import pallas as pl
from jax.experimental.pallas import tpu as pltpu

F32 = jnp.float32
BF16 = jnp.bfloat16

HEAD_DIM = 128
SB_HEADS = 8
DN_HEADS = 8
SB_WIDTH = SB_HEADS * HEAD_DIM
DN_WIDTH = DN_HEADS * HEAD_DIM
SHORT_CONV = 4
FFN_CONV = 3
EPS = 1e-6

VMEM_LIMIT_BYTES = 56 * 1024 * 1024
IN_PROJ_VMEM_LIMIT_BYTES = 60 * 1024 * 1024
SB_VMEM_LIMIT_BYTES = 60 * 1024 * 1024
SUBLANES = 8
BF16_SUBLANES = 16

IN_TM, IN_TN = 1024, 1792
IN_ROW_PARTS = 2
SB_TQ, SB_TK = 512, 256
SB_HEADS_PER_STEP = 8
LOG2E = 1.4426950408889634
DN_BLOCK = 256
DN_HEADS_PER_STEP = 8
OUT_TM = 512
OUT_ROW_PARTS = 2
FFN_TM, FFN_TF = 1024, 512
FFN_ROW_PARTS = 4


def _rms(x, gain):
    return x * lax.rsqrt(jnp.mean(x * x, axis=-1, keepdims=True) + EPS) * gain


def _dot(a, b):
    return jnp.dot(a, b, preferred_element_type=F32)


def _dot_nt(a, b):
    return lax.dot_general(a, b, (((1,), (1,)), ((), ())), preferred_element_type=F32)


def _dot_tn(a, b):
    return lax.dot_general(a, b, (((0,), (0,)), ((), ())), preferred_element_type=F32)


def _in_proj_kernel(x_ref, gain_ref, w_ref, cscale_ref, wba_ref, alog_ref, dtb_ref, *rest, tm, n_cast):
    cast_in = rest[:n_cast]
    proj_ref, gb_ref, gbt_ref = rest[n_cast:n_cast + 3]
    cast_out = rest[n_cast + 3:2 * n_cast + 3]
    xn_ref = rest[2 * n_cast + 3]
    j = pl.program_id(1)

    def convert_slabs():
        for src, dst in zip(cast_in, cast_out):
            dst[...] = src[...].astype(dst.dtype)

    def project(xn, rows):
        proj_ref[rows, :] = (_dot(xn, w_ref[...]) * cscale_ref[...]).astype(proj_ref.dtype)

    def gates(ba, rows):
        lane = lax.broadcasted_iota(jnp.int32, ba.shape, 1)
        beta = jax.nn.sigmoid(ba)
        g = -(jnp.exp(alog_ref[...]) * jax.nn.softplus(ba + dtb_ref[...]))
        r = lax.broadcasted_iota(jnp.int32, (DN_BLOCK, DN_BLOCK), 0)
        c = lax.broadcasted_iota(jnp.int32, (DN_BLOCK, DN_BLOCK), 1)
        tri = (r >= c).astype(BF16)
        g1 = g.astype(BF16)
        rem = g - g1.astype(F32)
        g2 = rem.astype(BF16)
        g3 = (rem - g2.astype(F32)).astype(BF16)
        gsplit = jnp.concatenate([g1, g2, g3], axis=1)
        gc = []
        for blk0 in range(0, ba.shape[0], DN_BLOCK):
            part = _dot(tri, gsplit[blk0:blk0 + DN_BLOCK, :])
            gc.append(part[:, 0:128] + part[:, 128:256] + part[:, 256:384])
        gb = jnp.where(lane < DN_HEADS, beta, jnp.concatenate(gc, axis=0))
        gb_ref[rows, :] = gb
        gbt_ref[:, rows] = gb.T

    @pl.when(j == 0)
    def _():
        nrows = tm // IN_ROW_PARTS
        for p in range(IN_ROW_PARTS):
            rows = slice(p * nrows, (p + 1) * nrows)
            xn = _rms(x_ref[rows, :], gain_ref[...]).astype(BF16)
            xn_ref[rows, :] = xn
            ba = _dot(xn, wba_ref[...])
            project(xn, rows)
            gates(ba, rows)
        convert_slabs()

    @pl.when(j > 0)
    def _():
        project(xn_ref[...], slice(None))
        convert_slabs()


def _in_proj(x2d, gain, w_all, n, w_ba, alog, dtb, later_weights, layer):
    m, d = x2d.shape
    tm, tn = IN_TM, IN_TN
    gi, gj = m // tm, n // tn
    steps = gi * gj
    cscale = jnp.where(jnp.arange(n) < SB_WIDTH, HEAD_DIM ** -0.5, 1.0).astype(F32)[None, :]

    def slab_specs(w):
        rows_total, cols = w.shape[1:]
        nslabs = next(s for s in (steps // k for k in (1, 2, 4, 8))
                      if rows_total % s == 0 and (rows_total // s) % BF16_SUBLANES == 0)
        rep, rows = steps // nslabs, rows_total // nslabs
        return (pl.BlockSpec((None, rows, cols), lambda i, j: (layer, (i * gj + j) // rep, 0)),
                pl.BlockSpec((rows, cols), lambda i, j: ((i * gj + j) // rep, 0)))

    specs = [slab_specs(w) for w in later_weights]
    outs = pl.pallas_call(
        functools.partial(_in_proj_kernel, tm=tm, n_cast=len(later_weights)),
        grid=(gi, gj),
        in_specs=[
            pl.BlockSpec((tm, d), lambda i, j: (i, 0)),
            pl.BlockSpec((1, d), lambda i, j: (0, 0)),
            pl.BlockSpec((d, tn), lambda i, j: (0, j)),
            pl.BlockSpec((1, tn), lambda i, j: (0, j)),
            pl.BlockSpec((d, 128), lambda i, j: (0, 0)),
            pl.BlockSpec((1, 128), lambda i, j: (0, 0)),
            pl.BlockSpec((1, 128), lambda i, j: (0, 0)),
        ] + [s[0] for s in specs],
        out_specs=[
            pl.BlockSpec((tm, tn), lambda i, j: (i, j)),
            pl.BlockSpec((tm, 128), lambda i, j: (i, 0)),
            pl.BlockSpec((128, tm), lambda i, j: (0, i)),
        ] + [s[1] for s in specs],
        out_shape=[
            jax.ShapeDtypeStruct((m, n), BF16),
            jax.ShapeDtypeStruct((m, 128), F32),
            jax.ShapeDtypeStruct((128, m), F32),
        ] + [jax.ShapeDtypeStruct(w.shape[1:], BF16) for w in later_weights],
        scratch_shapes=[pltpu.VMEM((tm, d), BF16)],
        compiler_params=pltpu.CompilerParams(
            dimension_semantics=("arbitrary", "arbitrary"),
            vmem_limit_bytes=IN_PROJ_VMEM_LIMIT_BYTES),
        name="in_proj",
    )(x2d, gain, w_all, cscale, w_ba, alog, dtb, *later_weights)
    return outs[0], outs[1], outs[2], outs[3:]


def _sb_kernel(q_ref, k_ref, v_ref, gain_ref, o_ref, acc_ref, run_ref, *, tq, tk, nh):
    qi = pl.program_id(2)
    r = lax.broadcasted_iota(jnp.int32, (tk, tk), 0)
    c = lax.broadcasted_iota(jnp.int32, (tk, tk), 1)
    suffix = (r >= c).astype(BF16)
    half = tk // 2
    heads = range(nh)
    cols = [slice(hh * HEAD_DIM, (hh + 1) * HEAD_DIM) for hh in heads]

    acc_ref[...] = jnp.zeros_like(acc_ref)
    run_ref[...] = jnp.zeros_like(run_ref)

    def tiles(kb, row0, nrows, causal):
        keys = pl.ds(kb * tk, tk)
        qrows = slice(row0, row0 + nrows)
        if causal:
            rr = lax.broadcasted_iota(jnp.int32, (nrows, tk), 0)
            cc = lax.broadcasted_iota(jnp.int32, (nrows, tk), 1)
            valid = cc < rr
        zs = [_dot_nt(q_ref[qrows, cols[hh]], k_ref[keys, cols[hh]]) for hh in heads]
        css = []
        for z in zs:
            sp = jnp.maximum(z, 0.0) + jnp.log(1.0 + jnp.exp2(jnp.abs(z) * (-LOG2E)))
            if causal:
                sp = jnp.where(valid, sp, 0.0)
            css.append(_dot(sp.astype(BF16), suffix))
        pvs, totals = [], []
        for hh in heads:
            z, cs, run = zs[hh], css[hh], run_ref[hh, qrows, :]
            args = [z[:, :half] - cs[:, :half] - run, z[:, half:] - cs[:, half:] - run]
            if causal:
                args = [jnp.where(valid[:, :half], args[0], -jnp.inf),
                        jnp.where(valid[:, half:], args[1], -jnp.inf)]
            att = jnp.concatenate([jnp.exp(a.astype(BF16)) for a in args], axis=1)
            pvs.append(_dot(att, v_ref[keys, cols[hh]]))
            totals.append(jnp.broadcast_to(cs[:, 0:1], (nrows, HEAD_DIM)))
        for hh in heads:
            acc_ref[hh, qrows, :] += pvs[hh]
            run_ref[hh, qrows, :] += totals[hh]

    nk = tq // tk
    for j in reversed(range(nk)):
        tiles(qi * nk + j, j * tk, tq - j * tk, True)

    def body(step, carry):
        tiles(qi * nk - 1 - step, 0, tq, False)
        return carry

    lax.fori_loop(0, qi * nk, body, 0)
    gain = gain_ref[...]
    for hh in range(nh):
        o_ref[:, hh * HEAD_DIM:(hh + 1) * HEAD_DIM] = _rms(acc_ref[hh], gain).astype(o_ref.dtype)


def _sb_attention(proj3, gain, *, q_col, k_col, v_col):
    b, t, _ = proj3.shape
    tq, tk, nh = SB_TQ, SB_TK, SB_HEADS_PER_STEP
    assert tk == 2 * HEAD_DIM and tq % tk == 0
    width = nh * HEAD_DIM
    whole = lambda col: pl.BlockSpec((None, t, width), lambda bi, g, i: (bi, 0, col // nh + g))
    return pl.pallas_call(
        functools.partial(_sb_kernel, tq=tq, tk=tk, nh=nh),
        grid=(b, SB_HEADS // nh, t // tq),
        in_specs=[
            pl.BlockSpec((None, tq, width), lambda bi, g, i: (bi, i, q_col // nh + g)),
            whole(k_col),
            whole(v_col),
            pl.BlockSpec((1, HEAD_DIM), lambda bi, g, i: (0, 0)),
        ],
        out_specs=pl.BlockSpec((None, tq, width), lambda bi, g, i: (bi, i, g)),
        out_shape=jax.ShapeDtypeStruct((b, t, SB_WIDTH), BF16),
        scratch_shapes=[
            pltpu.VMEM((nh, tq, HEAD_DIM), F32),
            pltpu.VMEM((nh, tq, HEAD_DIM), F32),
        ],
        compiler_params=pltpu.CompilerParams(
            dimension_semantics=("arbitrary", "arbitrary", "arbitrary"),
            vmem_limit_bytes=SB_VMEM_LIMIT_BYTES),
        name="sb_attn",
    )(proj3, proj3, proj3, gain)


def _dn_kernel(q_ref, k_ref, v_ref, z_ref, wq_ref, wk_ref, wv_ref, gb_ref, gbt_ref, gain_ref,
               o_ref, s_ref, bq_ref, bk_ref, bv_ref, *, blk, nh):
    grp = pl.program_id(1)
    i = pl.program_id(2)
    halo = SUBLANES
    heads = range(nh)
    cols = [slice(hh * HEAD_DIM, (hh + 1) * HEAD_DIM) for hh in heads]

    @pl.when(i == 0)
    def _():
        s_ref[...] = jnp.zeros_like(s_ref)
        for buf in (bq_ref, bk_ref, bv_ref):
            buf[0:halo, :] = jnp.zeros((halo, nh * HEAD_DIM), F32)

    def conv_silu(x_ref, w_ref, buf_ref):
        x = x_ref[...].astype(F32)
        buf_ref[halo:halo + blk, :] = x
        w = w_ref[...]
        y = x * w[3:4, :]
        for j in range(SHORT_CONV - 1):
            y = y + buf_ref[pl.ds(halo - (SHORT_CONV - 1) + j, blk), :] * w[j:j + 1, :]
        buf_ref[0:halo, :] = x[blk - halo:blk, :]
        return y * jax.nn.sigmoid(y)

    def l2n(x):
        return x * lax.rsqrt(jnp.sum(x * x, axis=-1, keepdims=True) + EPS)

    qc = conv_silu(q_ref, wq_ref, bq_ref)
    kc = conv_silu(k_ref, wk_ref, bk_ref)
    vc = conv_silu(v_ref, wv_ref, bv_ref)
    q = [l2n(qc[:, cols[hh]]) * (HEAD_DIM ** -0.5) for hh in heads]
    k = [l2n(kc[:, cols[hh]]) for hh in heads]
    v = [vc[:, cols[hh]] for hh in heads]

    gb = gb_ref[...]
    gbt = gbt_ref[...]
    lane = lax.broadcasted_iota(jnp.int32, gb.shape, 1)
    sub = lax.broadcasted_iota(jnp.int32, gbt.shape, 0)
    r = lax.broadcasted_iota(jnp.int32, (blk, blk), 0)
    c = lax.broadcasted_iota(jnp.int32, (blk, blk), 1)
    causal = r >= c

    beta, g_col, g_last, decay = [], [], [], []
    for hh in heads:
        head = grp * nh + hh
        beta.append(jnp.sum(jnp.where(lane == head, gb, 0.0), axis=1, keepdims=True))
        g_col.append(jnp.sum(jnp.where(lane == head + DN_HEADS, gb, 0.0), axis=1, keepdims=True))
        g_row = jnp.sum(jnp.where(sub == head + DN_HEADS, gbt, 0.0), axis=0, keepdims=True)
        g_last.append(g_row[:, blk - 1:blk])
        decay.append(jnp.where(causal, jnp.exp(jnp.where(causal, g_col[hh] - g_row, 0.0)), 0.0))

    k_b = [k[hh].astype(BF16) for hh in heads]
    kb = [k[hh] * beta[hh] for hh in heads]
    kk = [_dot_nt(kb[hh].astype(BF16), k_b[hh]) for hh in heads]
    qk = [_dot_nt(q[hh].astype(BF16), k_b[hh]) for hh in heads]
    a = [jnp.where(r > c, kk[hh] * decay[hh], 0.0) for hh in heads]
    qa = [(qk[hh] * decay[hh]).astype(BF16) for hh in heads]

    rc = r ^ c
    level = 31 - lax.clz(rc)
    eye = jnp.where(r == c, 1.0, 0.0)
    t = [eye - jnp.where(level == 0, a[hh], 0.0) for hh in heads]
    for p in range(1, blk.bit_length() - 1):
        t_b = [t[hh].astype(BF16) for hh in heads]
        cross = level == p
        prod = [_dot(jnp.where(cross, a[hh], 0.0).astype(BF16), t_b[hh]) for hh in heads]
        t = [t[hh] - _dot(t_b[hh], prod[hh].astype(BF16)) for hh in heads]

    eg = [jnp.exp(g_col[hh]) for hh in heads]
    t_b = [t[hh].astype(BF16) for hh in heads]
    u = [_dot(t_b[hh], (v[hh] * beta[hh]).astype(BF16)) for hh in heads]
    w = [_dot(t_b[hh], (kb[hh] * eg[hh]).astype(BF16)) for hh in heads]
    qd = [(q[hh] * eg[hh]).astype(BF16) for hh in heads]
    kt = [(k[hh] * jnp.exp(g_last[hh] - g_col[hh])).astype(BF16) for hh in heads]

    s = [s_ref[hh] for hh in heads]
    s_b = [s[hh].astype(BF16) for hh in heads]
    ws = [_dot(w[hh].astype(BF16), s_b[hh]) for hh in heads]
    qs = [_dot(qd[hh], s_b[hh]) for hh in heads]
    vn_b = [(u[hh] - ws[hh]).astype(BF16) for hh in heads]
    o = [qs[hh] + _dot(qa[hh], vn_b[hh]) for hh in heads]
    for hh in heads:
        s_ref[hh] = s[hh] * jnp.exp(g_last[hh]) + _dot_tn(kt[hh], vn_b[hh])

    gain = gain_ref[...]
    for hh in heads:
        zz = z_ref[:, cols[hh]].astype(F32)
        o_ref[:, cols[hh]] = (_rms(o[hh], gain) * (zz * jax.nn.sigmoid(zz))).astype(o_ref.dtype)


def _dn_scan(proj3, conv_w, gb3, gbt, gain, *, q_col, k_col, v_col, z_col):
    b, t, _ = proj3.shape
    blk, nh = DN_BLOCK, DN_HEADS_PER_STEP
    nblk = t // blk
    width = nh * HEAD_DIM
    tok = lambda col: pl.BlockSpec((None, blk, width), lambda bi, g, i: (bi, i, col // nh + g))
    cw = lambda col: pl.BlockSpec((SHORT_CONV, width), lambda bi, g, i: (0, col // nh + g))
    return pl.pallas_call(
        functools.partial(_dn_kernel, blk=blk, nh=nh),
        grid=(b, DN_HEADS // nh, nblk),
        in_specs=[
            tok(q_col), tok(k_col), tok(v_col), tok(z_col),
            cw(0), cw(DN_HEADS), cw(2 * DN_HEADS),
            pl.BlockSpec((None, blk, 128), lambda bi, g, i: (bi, i, 0)),
            pl.BlockSpec((128, blk), lambda bi, g, i: (0, bi * nblk + i)),
            pl.BlockSpec((1, HEAD_DIM), lambda bi, g, i: (0, 0)),
        ],
        out_specs=pl.BlockSpec((None, blk, width), lambda bi, g, i: (bi, i, g)),
        out_shape=jax.ShapeDtypeStruct((b, t, DN_WIDTH), BF16),
        scratch_shapes=[
            pltpu.VMEM((nh, HEAD_DIM, HEAD_DIM), F32),
            pltpu.VMEM((blk + SUBLANES, width), F32),
            pltpu.VMEM((blk + SUBLANES, width), F32),
            pltpu.VMEM((blk + SUBLANES, width), F32),
        ],
        compiler_params=pltpu.CompilerParams(
            dimension_semantics=("arbitrary", "arbitrary", "arbitrary"),
            vmem_limit_bytes=VMEM_LIMIT_BYTES),
        name="dn_scan",
    )(proj3, proj3, proj3, proj3, conv_w, conv_w, conv_w, gb3, gbt, gain)


def _out_proj_kernel(osb_ref, odn_ref, w1_ref, w2_ref, x_ref, gpost_ref, gpre_ref, h_ref, xn_ref):
    rows = osb_ref.shape[0] // OUT_ROW_PARTS
    for p in range(OUT_ROW_PARTS):
        rs = slice(p * rows, (p + 1) * rows)
        m = _dot(osb_ref[rs, :], w1_ref[...]) + _dot(odn_ref[rs, :], w2_ref[...])
        hres = x_ref[rs, :] + _rms(m, gpost_ref[...])
        h_ref[rs, :] = hres
        xn_ref[rs, :] = _rms(hres, gpre_ref[...]).astype(xn_ref.dtype)


def _out_proj(o_sb, o_dn, w_o, x2d, g_post, g_pre):
    m, d = x2d.shape
    tm = OUT_TM
    assert SB_WIDTH == DN_WIDTH
    row = lambda width: pl.BlockSpec((tm, width), lambda i: (i, 0))
    full = lambda a: pl.BlockSpec(a.shape, lambda i: (0, 0))
    w_half = lambda part: pl.BlockSpec((SB_WIDTH, d), lambda i: (part, 0))
    return pl.pallas_call(
        _out_proj_kernel,
        grid=(m // tm,),
        in_specs=[row(SB_WIDTH), row(DN_WIDTH), w_half(0), w_half(1), row(d), full(g_post), full(g_pre)],
        out_specs=[row(d), row(d)],
        out_shape=[jax.ShapeDtypeStruct((m, d), F32), jax.ShapeDtypeStruct((m, d), BF16)],
        compiler_params=pltpu.CompilerParams(
            dimension_semantics=("arbitrary",), vmem_limit_bytes=VMEM_LIMIT_BYTES),
        name="out_proj",
    )(o_sb, o_dn, w_o, w_o, x2d, g_post, g_pre)


def _ffn_kernel(xn_ref, wg_ref, wv_ref, cwg_ref, cwv_ref, bg_ref, bv_ref, wd_ref, gain_ref, h_hbm,
                y_hbm, acc_ref, bufg_ref, bufv_ref, tailg_ref, tailv_ref, hy_ref, h_sem, y_sem,
                *, tm, nf, tiles_per_seq):
    i = pl.program_id(0)
    f = pl.program_id(1)
    halo = SUBLANES
    parts = range(FFN_ROW_PARTS)
    rows = tm // FFN_ROW_PARTS
    seq_start = (i % tiles_per_seq) == 0

    def h_copy(tile):
        return pltpu.make_async_copy(h_hbm.at[pl.ds(tile * tm, tm), :], hy_ref, h_sem)

    def y_copy(tile):
        return pltpu.make_async_copy(hy_ref, y_hbm.at[pl.ds(tile * tm, tm), :], y_sem)

    @pl.when(f == 0)
    def _():
        acc_ref[...] = jnp.zeros_like(acc_ref)

    @pl.when(f == 1)
    def _():
        @pl.when(i > 0)
        def _():
            y_copy(i - 1).wait()
        h_copy(i).start()

    ups = []
    for p in parts:
        xn = xn_ref[p * rows:(p + 1) * rows, :]
        ups.append((_dot(xn, wg_ref[...]), _dot(xn, wv_ref[...])))

    def conv(p, which, cw_ref, b_ref, buf_ref, tail_ref):
        up = ups[p][which]
        prev = ups[p - 1][which][rows - halo:rows, :] if p else jnp.where(seq_start, 0.0, tail_ref[f])
        buf_ref[p, 0:halo, :] = prev
        buf_ref[p, halo:halo + rows, :] = up
        cw = cw_ref[...]
        out = up * cw[2:3, :] + b_ref[...]
        for j in range(FFN_CONV - 1):
            out = out + buf_ref[p, pl.ds(halo - (FFN_CONV - 1) + j, rows), :] * cw[j:j + 1, :]
        return out

    for p in parts:
        gate = conv(p, 0, cwg_ref, bg_ref, bufg_ref, tailg_ref)
        val = conv(p, 1, cwv_ref, bv_ref, bufv_ref, tailv_ref)
        act = (jax.nn.gelu(gate, approximate=True) * val).astype(BF16)
        acc_ref[p * rows:(p + 1) * rows, :] += _dot(act, wd_ref[...])
    tailg_ref[f] = ups[-1][0][rows - halo:rows, :]
    tailv_ref[f] = ups[-1][1][rows - halo:rows, :]

    @pl.when(f == nf - 1)
    def _():
        h_copy(i).wait()
        hy_ref[...] = hy_ref[...] + _rms(acc_ref[...], gain_ref[...])
        y_copy(i).start()

        @pl.when(i == pl.num_programs(0) - 1)
        def _():
            y_copy(i).wait()


def _conv_ffn(xn2, w_up, conv_w, conv_b, w_down, hres, gain, *, seq_len):
    m, d = hres.shape
    d_ff = w_down.shape[0]
    tm, tf = FFN_TM, FFN_TF
    nf = d_ff // tf
    assert nf >= 3
    tok = lambda i, f: (i, 0)
    return pl.pallas_call(
        functools.partial(_ffn_kernel, tm=tm, nf=nf, tiles_per_seq=seq_len // tm),
        grid=(m // tm, nf),
        in_specs=[
            pl.BlockSpec((tm, d), tok),
            pl.BlockSpec((d, tf), lambda i, f: (0, f)),
            pl.BlockSpec((d, tf), lambda i, f: (0, nf + f)),
            pl.BlockSpec((FFN_CONV, tf), lambda i, f: (0, f)),
            pl.BlockSpec((FFN_CONV, tf), lambda i, f: (0, nf + f)),
            pl.BlockSpec((1, tf), lambda i, f: (0, f)),
            pl.BlockSpec((1, tf), lambda i, f: (0, nf + f)),
            pl.BlockSpec((tf, d), lambda i, f: (f, 0)),
            pl.BlockSpec((1, d), lambda i, f: (0, 0)),
            pl.BlockSpec(memory_space=pl.ANY),
        ],
        out_specs=pl.BlockSpec(memory_space=pl.ANY),
        out_shape=jax.ShapeDtypeStruct((m, d), F32),
        scratch_shapes=[
            pltpu.VMEM((tm, d), F32),
            pltpu.VMEM((FFN_ROW_PARTS, tm // FFN_ROW_PARTS + SUBLANES, tf), F32),
            pltpu.VMEM((FFN_ROW_PARTS, tm // FFN_ROW_PARTS + SUBLANES, tf), F32),
            pltpu.VMEM((nf, SUBLANES, tf), F32),
            pltpu.VMEM((nf, SUBLANES, tf), F32),
            pltpu.VMEM((tm, d), F32),
            pltpu.SemaphoreType.DMA(()),
            pltpu.SemaphoreType.DMA(()),
        ],
        compiler_params=pltpu.CompilerParams(
            dimension_semantics=("arbitrary", "arbitrary"), vmem_limit_bytes=VMEM_LIMIT_BYTES),
        name="conv_ffn",
    )(xn2, w_up, w_up, conv_w, conv_w, conv_b, conv_b, w_down, gain, hres)


def _pad_lanes(vec):
    return jnp.pad(vec.astype(F32), (0, 128 - vec.shape[0]))[None, :]


def kernel(x, w_in, sb_out_gain, dn_conv_w, dn_a_log, dn_dt_bias, dn_out_gain, w_out, ln_mix_pre,
           ln_mix_post, w_up, ffn_conv_w, ffn_conv_b, w_down, ln_ffn_pre, ln_ffn_post):
    b, t, d = x.shape
    depth = w_in.shape[0]
    n_main = 3 * SB_WIDTH + 4 * DN_WIDTH
    assert t % DN_BLOCK == 0 and t % SB_TQ == 0 and t % FFN_TM == 0
    assert (b * t) % IN_TM == 0 and n_main % IN_TN == 0 and IN_TM % DN_BLOCK == 0
    hcols = lambda off: off // HEAD_DIM

    h2d = x.reshape(b * t, d)
    for l in range(depth):
        w_all = w_in[l].astype(BF16)
        w_ba = jnp.pad(w_in[l, :, n_main:], ((0, 0), (0, 128 - 2 * DN_HEADS))).astype(BF16)
        alog = _pad_lanes(jnp.concatenate([jnp.zeros((DN_HEADS,), F32), dn_a_log[l]]))
        dtb = _pad_lanes(jnp.concatenate([jnp.zeros((DN_HEADS,), F32), dn_dt_bias[l]]))
        proj, gb, gbt, (w_up_b, w_down_b, w_o) = _in_proj(
            h2d, ln_mix_pre[l][None, :], w_all, n_main, w_ba, alog, dtb, (w_up, w_down, w_out), l)
        proj3 = proj.reshape(b, t, n_main)

        o_sb = _sb_attention(proj3, sb_out_gain[l][None, :],
                             q_col=0, k_col=hcols(SB_WIDTH), v_col=hcols(2 * SB_WIDTH))
        dn0 = 3 * SB_WIDTH
        o_dn = _dn_scan(proj3, dn_conv_w[l], gb.reshape(b, t, 128), gbt, dn_out_gain[l][None, :],
                        q_col=hcols(dn0), k_col=hcols(dn0 + DN_WIDTH),
                        v_col=hcols(dn0 + 2 * DN_WIDTH), z_col=hcols(dn0 + 3 * DN_WIDTH))

        hres, xn2 = _out_proj(o_sb.reshape(b * t, SB_WIDTH), o_dn.reshape(b * t, DN_WIDTH),
                              w_o, h2d, ln_mix_post[l][None, :], ln_ffn_pre[l][None, :])

        h2d = _conv_ffn(xn2, w_up_b, ffn_conv_w[l], ffn_conv_b[l][None, :],
                        w_down_b, hres, ln_ffn_post[l][None, :], seq_len=t)
    return h2d.reshape(b, t, d)
```

```python
import functools

import jax
import jax.numpy as jnp
from jax import lax
from jax.experimental import pallas as pl
from jax.experimental.pallas import tpu as pltpu

F32 = jnp.float32
BF16 = jnp.bfloat16

HEAD_DIM = 128
SB_HEADS = 8
DN_HEADS = 8
SB_WIDTH = SB_HEADS * HEAD_DIM
DN_WIDTH = DN_HEADS * HEAD_DIM
SHORT_CONV = 4
FFN_CONV = 3
EPS = 1e-6

VMEM_LIMIT_BYTES = 56 * 1024 * 1024
IN_PROJ_VMEM_LIMIT_BYTES = 60 * 1024 * 1024
SB_VMEM_LIMIT_BYTES = 60 * 1024 * 1024
SUBLANES = 8
BF16_SUBLANES = 16

IN_TM, IN_TN = 1024, 1792
IN_ROW_PARTS = 2
SB_TQ, SB_TK = 512, 256
SB_HEADS_PER_STEP = 8
LOG2E = 1.4426950408889634
DN_BLOCK = 256
DN_HEADS_PER_STEP = 8
OUT_TM = 512
OUT_ROW_PARTS = 2
FFN_TM, FFN_TF = 1024, 512
FFN_ROW_PARTS = 4


def _rms(x, gain):
    return x * lax.rsqrt(jnp.mean(x * x, axis=-1, keepdims=True) + EPS) * gain


def _dot(a, b):
    return jnp.dot(a, b, preferred_element_type=F32)


def _dot_nt(a, b):
    return lax.dot_general(a, b, (((1,), (1,)), ((), ())), preferred_element_type=F32)


def _dot_tn(a, b):
    return lax.dot_general(a, b, (((0,), (0,)), ((), ())), preferred_element_type=F32)


def _in_proj_kernel(x_ref, gain_ref, w_ref, cscale_ref, wba_ref, alog_ref, dtb_ref, *rest, tm, n_cast):
    cast_in = rest[:n_cast]
    proj_ref, gb_ref, gbt_ref = rest[n_cast:n_cast + 3]
    cast_out = rest[n_cast + 3:2 * n_cast + 3]
    xn_ref = rest[2 * n_cast + 3]
    j = pl.program_id(1)

    def convert_slabs():
        for src, dst in zip(cast_in, cast_out):
            dst[...] = src[...].astype(dst.dtype)

    def project(xn, rows):
        proj_ref[rows, :] = (_dot(xn, w_ref[...]) * cscale_ref[...]).astype(proj_ref.dtype)

    def gates(ba, rows):
        lane = lax.broadcasted_iota(jnp.int32, ba.shape, 1)
        beta = jax.nn.sigmoid(ba)
        g = -(jnp.exp(alog_ref[...]) * jax.nn.softplus(ba + dtb_ref[...]))
        r = lax.broadcasted_iota(jnp.int32, (DN_BLOCK, DN_BLOCK), 0)
        c = lax.broadcasted_iota(jnp.int32, (DN_BLOCK, DN_BLOCK), 1)
        tri = (r >= c).astype(BF16)
        g1 = g.astype(BF16)
        rem = g - g1.astype(F32)
        g2 = rem.astype(BF16)
        g3 = (rem - g2.astype(F32)).astype(BF16)
        gsplit = jnp.concatenate([g1, g2, g3], axis=1)
        gc = []
        for blk0 in range(0, ba.shape[0], DN_BLOCK):
            part = _dot(tri, gsplit[blk0:blk0 + DN_BLOCK, :])
            gc.append(part[:, 0:128] + part[:, 128:256] + part[:, 256:384])
        gb = jnp.where(lane < DN_HEADS, beta, jnp.concatenate(gc, axis=0))
        gb_ref[rows, :] = gb
        gbt_ref[:, rows] = gb.T

    @pl.when(j == 0)
    def _():
        nrows = tm // IN_ROW_PARTS
        for p in range(IN_ROW_PARTS):
            rows = slice(p * nrows, (p + 1) * nrows)
            xn = _rms(x_ref[rows, :], gain_ref[...]).astype(BF16)
            xn_ref[rows, :] = xn
            ba = _dot(xn, wba_ref[...])
            project(xn, rows)
            gates(ba, rows)
        convert_slabs()

    @pl.when(j > 0)
    def _():
        project(xn_ref[...], slice(None))
        convert_slabs()


def _in_proj(x2d, gain, w_all, n, w_ba, alog, dtb, later_weights, layer):
    m, d = x2d.shape
    tm, tn = IN_TM, IN_TN
    gi, gj = m // tm, n // tn
    steps = gi * gj
    cscale = jnp.where(jnp.arange(n) < SB_WIDTH, HEAD_DIM ** -0.5, 1.0).astype(F32)[None, :]

    def slab_specs(w):
        rows_total, cols = w.shape[1:]
        nslabs = next(s for s in (steps // k for k in (1, 2, 4, 8))
                      if rows_total % s == 0 and (rows_total // s) % BF16_SUBLANES == 0)
        rep, rows = steps // nslabs, rows_total // nslabs
        return (pl.BlockSpec((None, rows, cols), lambda i, j: (layer, (i * gj + j) // rep, 0)),
                pl.BlockSpec((rows, cols), lambda i, j: ((i * gj + j) // rep, 0)))

    specs = [slab_specs(w) for w in later_weights]
    outs = pl.pallas_call(
        functools.partial(_in_proj_kernel, tm=tm, n_cast=len(later_weights)),
        grid=(gi, gj),
        in_specs=[
            pl.BlockSpec((tm, d), lambda i, j: (i, 0)),
            pl.BlockSpec((1, d), lambda i, j: (0, 0)),
            pl.BlockSpec((d, tn), lambda i, j: (0, j)),
            pl.BlockSpec((1, tn), lambda i, j: (0, j)),
            pl.BlockSpec((d, 128), lambda i, j: (0, 0)),
            pl.BlockSpec((1, 128), lambda i, j: (0, 0)),
            pl.BlockSpec((1, 128), lambda i, j: (0, 0)),
        ] + [s[0] for s in specs],
        out_specs=[
            pl.BlockSpec((tm, tn), lambda i, j: (i, j)),
            pl.BlockSpec((tm, 128), lambda i, j: (i, 0)),
            pl.BlockSpec((128, tm), lambda i, j: (0, i)),
        ] + [s[1] for s in specs],
        out_shape=[
            jax.ShapeDtypeStruct((m, n), BF16),
            jax.ShapeDtypeStruct((m, 128), F32),
            jax.ShapeDtypeStruct((128, m), F32),
        ] + [jax.ShapeDtypeStruct(w.shape[1:], BF16) for w in later_weights],
        scratch_shapes=[pltpu.VMEM((tm, d), BF16)],
        compiler_params=pltpu.CompilerParams(
            dimension_semantics=("arbitrary", "arbitrary"),
            vmem_limit_bytes=IN_PROJ_VMEM_LIMIT_BYTES),
        name="in_proj",
    )(x2d, gain, w_all, cscale, w_ba, alog, dtb, *later_weights)
    return outs[0], outs[1], outs[2], outs[3:]


def _sb_kernel(q_ref, k_ref, v_ref, gain_ref, o_ref, acc_ref, run_ref, *, tq, tk, nh):
    qi = pl.program_id(2)
    r = lax.broadcasted_iota(jnp.int32, (tk, tk), 0)
    c = lax.broadcasted_iota(jnp.int32, (tk, tk), 1)
    suffix = (r >= c).astype(BF16)
    half = tk // 2
    heads = range(nh)
    cols = [slice(hh * HEAD_DIM, (hh + 1) * HEAD_DIM) for hh in heads]

    acc_ref[...] = jnp.zeros_like(acc_ref)
    run_ref[...] = jnp.zeros_like(run_ref)

    def tiles(kb, row0, nrows, causal):
        keys = pl.ds(kb * tk, tk)
        qrows = slice(row0, row0 + nrows)
        if causal:
            rr = lax.broadcasted_iota(jnp.int32, (nrows, tk), 0)
            cc = lax.broadcasted_iota(jnp.int32, (nrows, tk), 1)
            valid = cc < rr
        zs = [_dot_nt(q_ref[qrows, cols[hh]], k_ref[keys, cols[hh]]) for hh in heads]
        css = []
        for z in zs:
            sp = jnp.maximum(z, 0.0) + jnp.log(1.0 + jnp.exp2(jnp.abs(z) * (-LOG2E)))
            if causal:
                sp = jnp.where(valid, sp, 0.0)
            css.append(_dot(sp.astype(BF16), suffix))
        pvs, totals = [], []
        for hh in heads:
            z, cs, run = zs[hh], css[hh], run_ref[hh, qrows, :]
            args = [z[:, :half] - cs[:, :half] - run, z[:, half:] - cs[:, half:] - run]
            if causal:
                args = [jnp.where(valid[:, :half], args[0], -jnp.inf),
                        jnp.where(valid[:, half:], args[1], -jnp.inf)]
            att = jnp.concatenate([jnp.exp(a.astype(BF16)) for a in args], axis=1)
            pvs.append(_dot(att, v_ref[keys, cols[hh]]))
            totals.append(jnp.broadcast_to(cs[:, 0:1], (nrows, HEAD_DIM)))
        for hh in heads:
            acc_ref[hh, qrows, :] += pvs[hh]
            run_ref[hh, qrows, :] += totals[hh]

    nk = tq // tk
    for j in reversed(range(nk)):
        tiles(qi * nk + j, j * tk, tq - j * tk, True)

    def body(step, carry):
        tiles(qi * nk - 1 - step, 0, tq, False)
        return carry

    lax.fori_loop(0, qi * nk, body, 0)
    gain = gain_ref[...]
    for hh in range(nh):
        o_ref[:, hh * HEAD_DIM:(hh + 1) * HEAD_DIM] = _rms(acc_ref[hh], gain).astype(o_ref.dtype)


def _sb_attention(proj3, gain, *, q_col, k_col, v_col):
    b, t, _ = proj3.shape
    tq, tk, nh = SB_TQ, SB_TK, SB_HEADS_PER_STEP
    assert tk == 2 * HEAD_DIM and tq % tk == 0
    width = nh * HEAD_DIM
    whole = lambda col: pl.BlockSpec((None, t, width), lambda bi, g, i: (bi, 0, col // nh + g))
    return pl.pallas_call(
        functools.partial(_sb_kernel, tq=tq, tk=tk, nh=nh),
        grid=(b, SB_HEADS // nh, t // tq),
        in_specs=[
            pl.BlockSpec((None, tq, width), lambda bi, g, i: (bi, i, q_col // nh + g)),
            whole(k_col),
            whole(v_col),
            pl.BlockSpec((1, HEAD_DIM), lambda bi, g, i: (0, 0)),
        ],
        out_specs=pl.BlockSpec((None, tq, width), lambda bi, g, i: (bi, i, g)),
        out_shape=jax.ShapeDtypeStruct((b, t, SB_WIDTH), BF16),
        scratch_shapes=[
            pltpu.VMEM((nh, tq, HEAD_DIM), F32),
            pltpu.VMEM((nh, tq, HEAD_DIM), F32),
        ],
        compiler_params=pltpu.CompilerParams(
            dimension_semantics=("arbitrary", "arbitrary", "arbitrary"),
            vmem_limit_bytes=SB_VMEM_LIMIT_BYTES),
        name="sb_attn",
    )(proj3, proj3, proj3, gain)


def _dn_kernel(q_ref, k_ref, v_ref, z_ref, wq_ref, wk_ref, wv_ref, gb_ref, gbt_ref, gain_ref,
               o_ref, s_ref, bq_ref, bk_ref, bv_ref, *, blk, nh):
    grp = pl.program_id(1)
    i = pl.program_id(2)
    halo = SUBLANES
    heads = range(nh)
    cols = [slice(hh * HEAD_DIM, (hh + 1) * HEAD_DIM) for hh in heads]

    @pl.when(i == 0)
    def _():
        s_ref[...] = jnp.zeros_like(s_ref)
        for buf in (bq_ref, bk_ref, bv_ref):
            buf[0:halo, :] = jnp.zeros((halo, nh * HEAD_DIM), F32)

    def conv_silu(x_ref, w_ref, buf_ref):
        x = x_ref[...].astype(F32)
        buf_ref[halo:halo + blk, :] = x
        w = w_ref[...]
        y = x * w[3:4, :]
        for j in range(SHORT_CONV - 1):
            y = y + buf_ref[pl.ds(halo - (SHORT_CONV - 1) + j, blk), :] * w[j:j + 1, :]
        buf_ref[0:halo, :] = x[blk - halo:blk, :]
        return y * jax.nn.sigmoid(y)

    def l2n(x):
        return x * lax.rsqrt(jnp.sum(x * x, axis=-1, keepdims=True) + EPS)

    qc = conv_silu(q_ref, wq_ref, bq_ref)
    kc = conv_silu(k_ref, wk_ref, bk_ref)
    vc = conv_silu(v_ref, wv_ref, bv_ref)
    q = [l2n(qc[:, cols[hh]]) * (HEAD_DIM ** -0.5) for hh in heads]
    k = [l2n(kc[:, cols[hh]]) for hh in heads]
    v = [vc[:, cols[hh]] for hh in heads]

    gb = gb_ref[...]
    gbt = gbt_ref[...]
    lane = lax.broadcasted_iota(jnp.int32, gb.shape, 1)
    sub = lax.broadcasted_iota(jnp.int32, gbt.shape, 0)
    r = lax.broadcasted_iota(jnp.int32, (blk, blk), 0)
    c = lax.broadcasted_iota(jnp.int32, (blk, blk), 1)
    causal = r >= c

    beta, g_col, g_last, decay = [], [], [], []
    for hh in heads:
        head = grp * nh + hh
        beta.append(jnp.sum(jnp.where(lane == head, gb, 0.0), axis=1, keepdims=True))
        g_col.append(jnp.sum(jnp.where(lane == head + DN_HEADS, gb, 0.0), axis=1, keepdims=True))
        g_row = jnp.sum(jnp.where(sub == head + DN_HEADS, gbt, 0.0), axis=0, keepdims=True)
        g_last.append(g_row[:, blk - 1:blk])
        decay.append(jnp.where(causal, jnp.exp(jnp.where(causal, g_col[hh] - g_row, 0.0)), 0.0))

    k_b = [k[hh].astype(BF16) for hh in heads]
    kb = [k[hh] * beta[hh] for hh in heads]
    kk = [_dot_nt(kb[hh].astype(BF16), k_b[hh]) for hh in heads]
    qk = [_dot_nt(q[hh].astype(BF16), k_b[hh]) for hh in heads]
    a = [jnp.where(r > c, kk[hh] * decay[hh], 0.0) for hh in heads]
    qa = [(qk[hh] * decay[hh]).astype(BF16) for hh in heads]

    rc = r ^ c
    level = 31 - lax.clz(rc)
    eye = jnp.where(r == c, 1.0, 0.0)
    t = [eye - jnp.where(level == 0, a[hh], 0.0) for hh in heads]
    top = blk.bit_length() - 2
    for p in range(1, top):
        t_b = [t[hh].astype(BF16) for hh in heads]
        cross = level == p
        prod = [_dot(jnp.where(cross, a[hh], 0.0).astype(BF16), t_b[hh]) for hh in heads]
        t = [t[hh] - _dot(t_b[hh], prod[hh].astype(BF16)) for hh in heads]

    eg = [jnp.exp(g_col[hh]) for hh in heads]
    t_b = [t[hh].astype(BF16) for hh in heads]
    rhs = [jnp.concatenate([v[hh] * beta[hh], kb[hh] * eg[hh]], axis=1).astype(BF16) for hh in heads]
    y = [_dot(t_b[hh], rhs[hh]) for hh in heads]
    cy = [_dot(jnp.where(level == top, a[hh], 0.0).astype(BF16), y[hh].astype(BF16)) for hh in heads]
    x = [y[hh] - _dot(t_b[hh], cy[hh].astype(BF16)) for hh in heads]
    u = [x[hh][:, :HEAD_DIM] for hh in heads]
    w = [x[hh][:, HEAD_DIM:] for hh in heads]
    qd = [(q[hh] * eg[hh]).astype(BF16) for hh in heads]
    kt = [(k[hh] * jnp.exp(g_last[hh] - g_col[hh])).astype(BF16) for hh in heads]

    s = [s_ref[hh] for hh in heads]
    s_b = [s[hh].astype(BF16) for hh in heads]
    ws = [_dot(w[hh].astype(BF16), s_b[hh]) for hh in heads]
    qs = [_dot(qd[hh], s_b[hh]) for hh in heads]
    vn_b = [(u[hh] - ws[hh]).astype(BF16) for hh in heads]
    o = [qs[hh] + _dot(qa[hh], vn_b[hh]) for hh in heads]
    for hh in heads:
        s_ref[hh] = s[hh] * jnp.exp(g_last[hh]) + _dot_tn(kt[hh], vn_b[hh])

    gain = gain_ref[...]
    for hh in heads:
        zz = z_ref[:, cols[hh]].astype(F32)
        o_ref[:, cols[hh]] = (_rms(o[hh], gain) * (zz * jax.nn.sigmoid(zz))).astype(o_ref.dtype)


def _dn_scan(proj3, conv_w, gb3, gbt, gain, *, q_col, k_col, v_col, z_col):
    b, t, _ = proj3.shape
    blk, nh = DN_BLOCK, DN_HEADS_PER_STEP
    nblk = t // blk
    width = nh * HEAD_DIM
    tok = lambda col: pl.BlockSpec((None, blk, width), lambda bi, g, i: (bi, i, col // nh + g))
    cw = lambda col: pl.BlockSpec((SHORT_CONV, width), lambda bi, g, i: (0, col // nh + g))
    return pl.pallas_call(
        functools.partial(_dn_kernel, blk=blk, nh=nh),
        grid=(b, DN_HEADS // nh, nblk),
        in_specs=[
            tok(q_col), tok(k_col), tok(v_col), tok(z_col),
            cw(0), cw(DN_HEADS), cw(2 * DN_HEADS),
            pl.BlockSpec((None, blk, 128), lambda bi, g, i: (bi, i, 0)),
            pl.BlockSpec((128, blk), lambda bi, g, i: (0, bi * nblk + i)),
            pl.BlockSpec((1, HEAD_DIM), lambda bi, g, i: (0, 0)),
        ],
        out_specs=pl.BlockSpec((None, blk, width), lambda bi, g, i: (bi, i, g)),
        out_shape=jax.ShapeDtypeStruct((b, t, DN_WIDTH), BF16),
        scratch_shapes=[
            pltpu.VMEM((nh, HEAD_DIM, HEAD_DIM), F32),
            pltpu.VMEM((blk + SUBLANES, width), F32),
            pltpu.VMEM((blk + SUBLANES, width), F32),
            pltpu.VMEM((blk + SUBLANES, width), F32),
        ],
        compiler_params=pltpu.CompilerParams(
            dimension_semantics=("arbitrary", "arbitrary", "arbitrary"),
            vmem_limit_bytes=VMEM_LIMIT_BYTES),
        name="dn_scan",
    )(proj3, proj3, proj3, proj3, conv_w, conv_w, conv_w, gb3, gbt, gain)


def _out_proj_kernel(osb_ref, odn_ref, w1_ref, w2_ref, x_ref, gpost_ref, gpre_ref, h_ref, xn_ref):
    rows = osb_ref.shape[0] // OUT_ROW_PARTS
    for p in range(OUT_ROW_PARTS):
        rs = slice(p * rows, (p + 1) * rows)
        m = _dot(osb_ref[rs, :], w1_ref[...]) + _dot(odn_ref[rs, :], w2_ref[...])
        hres = x_ref[rs, :] + _rms(m, gpost_ref[...])
        h_ref[rs, :] = hres
        xn_ref[rs, :] = _rms(hres, gpre_ref[...]).astype(xn_ref.dtype)


def _out_proj(o_sb, o_dn, w_o, x2d, g_post, g_pre):
    m, d = x2d.shape
    tm = OUT_TM
    assert SB_WIDTH == DN_WIDTH
    row = lambda width: pl.BlockSpec((tm, width), lambda i: (i, 0))
    full = lambda a: pl.BlockSpec(a.shape, lambda i: (0, 0))
    w_half = lambda part: pl.BlockSpec((SB_WIDTH, d), lambda i: (part, 0))
    return pl.pallas_call(
        _out_proj_kernel,
        grid=(m // tm,),
        in_specs=[row(SB_WIDTH), row(DN_WIDTH), w_half(0), w_half(1), row(d), full(g_post), full(g_pre)],
        out_specs=[row(d), row(d)],
        out_shape=[jax.ShapeDtypeStruct((m, d), F32), jax.ShapeDtypeStruct((m, d), BF16)],
        compiler_params=pltpu.CompilerParams(
            dimension_semantics=("arbitrary",), vmem_limit_bytes=VMEM_LIMIT_BYTES),
        name="out_proj",
    )(o_sb, o_dn, w_o, w_o, x2d, g_post, g_pre)


def _ffn_kernel(xn_ref, wg_ref, wv_ref, cwg_ref, cwv_ref, bg_ref, bv_ref, wd_ref, gain_ref, h_hbm,
                y_hbm, acc_ref, bufg_ref, bufv_ref, tailg_ref, tailv_ref, hy_ref, h_sem, y_sem,
                *, tm, nf, tiles_per_seq):
    i = pl.program_id(0)
    f = pl.program_id(1)
    halo = SUBLANES
    parts = range(FFN_ROW_PARTS)
    rows = tm // FFN_ROW_PARTS
    seq_start = (i % tiles_per_seq) == 0

    def h_copy(tile):
        return pltpu.make_async_copy(h_hbm.at[pl.ds(tile * tm, tm), :], hy_ref, h_sem)

    def y_copy(tile):
        return pltpu.make_async_copy(hy_ref, y_hbm.at[pl.ds(tile * tm, tm), :], y_sem)

    @pl.when(f == 0)
    def _():
        acc_ref[...] = jnp.zeros_like(acc_ref)

    @pl.when(f == 1)
    def _():
        @pl.when(i > 0)
        def _():
            y_copy(i - 1).wait()
        h_copy(i).start()

    ups = []
    for p in parts:
        xn = xn_ref[p * rows:(p + 1) * rows, :]
        ups.append((_dot(xn, wg_ref[...]), _dot(xn, wv_ref[...])))

    def conv(p, which, cw_ref, b_ref, buf_ref, tail_ref):
        up = ups[p][which]
        prev = ups[p - 1][which][rows - halo:rows, :] if p else jnp.where(seq_start, 0.0, tail_ref[f])
        buf_ref[p, 0:halo, :] = prev
        buf_ref[p, halo:halo + rows, :] = up
        cw = cw_ref[...]
        out = up * cw[2:3, :] + b_ref[...]
        for j in range(FFN_CONV - 1):
            out = out + buf_ref[p, pl.ds(halo - (FFN_CONV - 1) + j, rows), :] * cw[j:j + 1, :]
        return out

    for p in parts:
        gate = conv(p, 0, cwg_ref, bg_ref, bufg_ref, tailg_ref)
        val = conv(p, 1, cwv_ref, bv_ref, bufv_ref, tailv_ref)
        act = (jax.nn.gelu(gate, approximate=True) * val).astype(BF16)
        acc_ref[p * rows:(p + 1) * rows, :] += _dot(act, wd_ref[...])
    tailg_ref[f] = ups[-1][0][rows - halo:rows, :]
    tailv_ref[f] = ups[-1][1][rows - halo:rows, :]

    @pl.when(f == nf - 1)
    def _():
        h_copy(i).wait()
        hy_ref[...] = hy_ref[...] + _rms(acc_ref[...], gain_ref[...])
        y_copy(i).start()

        @pl.when(i == pl.num_programs(0) - 1)
        def _():
            y_copy(i).wait()


def _conv_ffn(xn2, w_up, conv_w, conv_b, w_down, hres, gain, *, seq_len):
    m, d = hres.shape
    d_ff = w_down.shape[0]
    tm, tf = FFN_TM, FFN_TF
    nf = d_ff // tf
    assert nf >= 3
    tok = lambda i, f: (i, 0)
    return pl.pallas_call(
        functools.partial(_ffn_kernel, tm=tm, nf=nf, tiles_per_seq=seq_len // tm),
        grid=(m // tm, nf),
        in_specs=[
            pl.BlockSpec((tm, d), tok),
            pl.BlockSpec((d, tf), lambda i, f: (0, f)),
            pl.BlockSpec((d, tf), lambda i, f: (0, nf + f)),
            pl.BlockSpec((FFN_CONV, tf), lambda i, f: (0, f)),
            pl.BlockSpec((FFN_CONV, tf), lambda i, f: (0, nf + f)),
            pl.BlockSpec((1, tf), lambda i, f: (0, f)),
            pl.BlockSpec((1, tf), lambda i, f: (0, nf + f)),
            pl.BlockSpec((tf, d), lambda i, f: (f, 0)),
            pl.BlockSpec((1, d), lambda i, f: (0, 0)),
            pl.BlockSpec(memory_space=pl.ANY),
        ],
        out_specs=pl.BlockSpec(memory_space=pl.ANY),
        out_shape=jax.ShapeDtypeStruct((m, d), F32),
        scratch_shapes=[
            pltpu.VMEM((tm, d), F32),
            pltpu.VMEM((FFN_ROW_PARTS, tm // FFN_ROW_PARTS + SUBLANES, tf), F32),
            pltpu.VMEM((FFN_ROW_PARTS, tm // FFN_ROW_PARTS + SUBLANES, tf), F32),
            pltpu.VMEM((nf, SUBLANES, tf), F32),
            pltpu.VMEM((nf, SUBLANES, tf), F32),
            pltpu.VMEM((tm, d), F32),
            pltpu.SemaphoreType.DMA(()),
            pltpu.SemaphoreType.DMA(()),
        ],
        compiler_params=pltpu.CompilerParams(
            dimension_semantics=("arbitrary", "arbitrary"), vmem_limit_bytes=VMEM_LIMIT_BYTES),
        name="conv_ffn",
    )(xn2, w_up, w_up, conv_w, conv_w, conv_b, conv_b, w_down, gain, hres)


def _pad_lanes(vec):
    return jnp.pad(vec.astype(F32), (0, 128 - vec.shape[0]))[None, :]


def kernel(x, w_in, sb_out_gain, dn_conv_w, dn_a_log, dn_dt_bias, dn_out_gain, w_out, ln_mix_pre,
           ln_mix_post, w_up, ffn_conv_w, ffn_conv_b, w_down, ln_ffn_pre, ln_ffn_post):
    b, t, d = x.shape
    depth = w_in.shape[0]
    n_main = 3 * SB_WIDTH + 4 * DN_WIDTH
    assert t % DN_BLOCK == 0 and t % SB_TQ == 0 and t % FFN_TM == 0
    assert (b * t) % IN_TM == 0 and n_main % IN_TN == 0 and IN_TM % DN_BLOCK == 0
    hcols = lambda off: off // HEAD_DIM

    h2d = x.reshape(b * t, d)
    for l in range(depth):
        w_all = w_in[l].astype(BF16)
        w_ba = jnp.pad(w_in[l, :, n_main:], ((0, 0), (0, 128 - 2 * DN_HEADS))).astype(BF16)
        alog = _pad_lanes(jnp.concatenate([jnp.zeros((DN_HEADS,), F32), dn_a_log[l]]))
        dtb = _pad_lanes(jnp.concatenate([jnp.zeros((DN_HEADS,), F32), dn_dt_bias[l]]))
        proj, gb, gbt, (w_up_b, w_down_b, w_o) = _in_proj(
            h2d, ln_mix_pre[l][None, :], w_all, n_main, w_ba, alog, dtb, (w_up, w_down, w_out), l)
        proj3 = proj.reshape(b, t, n_main)

        o_sb = _sb_attention(proj3, sb_out_gain[l][None, :],
                             q_col=0, k_col=hcols(SB_WIDTH), v_col=hcols(2 * SB_WIDTH))
        dn0 = 3 * SB_WIDTH
        o_dn = _dn_scan(proj3, dn_conv_w[l], gb.reshape(b, t, 128), gbt, dn_out_gain[l][None, :],
                        q_col=hcols(dn0), k_col=hcols(dn0 + DN_WIDTH),
                        v_col=hcols(dn0 + 2 * DN_WIDTH), z_col=hcols(dn0 + 3 * DN_WIDTH))

        hres, xn2 = _out_proj(o_sb.reshape(b * t, SB_WIDTH), o_dn.reshape(b * t, DN_WIDTH),
                              w_o, h2d, ln_mix_post[l][None, :], ln_ffn_pre[l][None, :])

        h2d = _conv_ffn(xn2, w_up_b, ffn_conv_w[l], ffn_conv_b[l][None, :],
                        w_down_b, hres, ln_ffn_post[l][None, :], seq_len=t)
    return h2d.reshape(b, t, d)
```
